```python
import math
import jax, jax.numpy as jnp
from jax import lax
import numpy as np

D_MODEL = 1024
BATCH = 8
SEQ = 2048
DEPTH = 2
DEC_BATCH = 16
DEC_SEQ = 64
PAST_LEN = 2048

CHUNK = 64
D_MIX = D_MODEL
D_ATTN = D_MIX // 2
D_CONV = D_MIX - D_ATTN
N_DIFF_HEADS = 4
N_SUB_HEADS = 2 * N_DIFF_HEADS
HEAD_DIM = D_ATTN // N_SUB_HEADS
V_DIM = 2 * HEAD_DIM
CONV_WIDTH = 31
CONV_HIST = CONV_WIDTH - 1
D_FF = 4 * D_MODEL
D_IN_PROJ = 3 * D_ATTN + 2 * D_CONV
ROPE_THETA = 10000.0
Q_BLOCK = 128
LN_EPS = 1e-5
RMS_EPS = 1e-5
DEEPNORM_ALPHA = (2.0 * DEPTH) ** 0.25
DEEPNORM_BETA = (8.0 * DEPTH) ** -0.25

kernel_name = "hymba_diffattn_conformer_stream_step"


def lambda_init(layer):
    return 0.8 - 0.6 * math.exp(-0.3 * layer)


def layer_norm(x, g, b):
    xf = x.astype(jnp.float32)
    mu = jnp.mean(xf, axis=-1, keepdims=True)
    var = jnp.mean(jnp.square(xf - mu), axis=-1, keepdims=True)
    return ((xf - mu) * lax.rsqrt(var + LN_EPS) * g + b).astype(x.dtype)


def rope(x, pos):
    half = HEAD_DIM // 2
    inv = ROPE_THETA ** (-jnp.arange(half, dtype=jnp.float32) / half)
    ang = pos.astype(jnp.float32)[:, None] * inv[None, :]
    cos = jnp.cos(ang)[None, :, None, :]
    sin = jnp.sin(ang)[None, :, None, :]
    xf = x.astype(jnp.float32)
    x1, x2 = xf[..., :half], xf[..., half:]
    return jnp.concatenate([x1 * cos - x2 * sin, x2 * cos + x1 * sin], axis=-1).astype(x.dtype)


def diff_core(q, k, v, mask, lam, sub_g, lam_init):
    B, Tq = q.shape[0], q.shape[1]
    Tk = k.shape[1]
    s = jnp.einsum('bqsd,bksd->bsqk', q, k).astype(jnp.float32) * (HEAD_DIM ** -0.5)
    if mask is not None:
        s = jnp.where(mask[None, None], s, -1e30)
    p = jax.nn.softmax(s, axis=-1).reshape(B, N_DIFF_HEADS, 2, Tq, Tk)
    a = p[:, :, 0] - lam * p[:, :, 1]
    o = jnp.einsum('bhqk,bkhe->bqhe', a, v.astype(jnp.float32))
    o = o * lax.rsqrt(jnp.mean(jnp.square(o), axis=-1, keepdims=True) + RMS_EPS)
    return (o * sub_g * (1.0 - lam_init)).astype(q.dtype)


def prompt_diff_attention(q, k, v, lam, sub_g, lam_init):
    B, T = q.shape[0], q.shape[1]
    nb = T // Q_BLOCK
    qb = q.reshape(B, nb, Q_BLOCK, N_SUB_HEADS, HEAD_DIM).transpose(1, 0, 2, 3, 4)
    key_chunk = jnp.arange(T) // CHUNK

    def one_block(args):
        q_i, i = args
        q_chunk = (i * Q_BLOCK + jnp.arange(Q_BLOCK)) // CHUNK
        mask = key_chunk[None, :] <= q_chunk[:, None]
        return diff_core(q_i, k, v, mask, lam, sub_g, lam_init)

    out = lax.map(one_block, (qb, jnp.arange(nb)))
    return out.transpose(1, 0, 2, 3, 4).reshape(B, T, N_DIFF_HEADS, V_DIM)


def causal_depthwise_conv(u, hist, w, b):
    full = jnp.concatenate([hist, u], axis=1)
    y = lax.conv_general_dilated(full, w[:, None, :], window_strides=(1,), padding='VALID',
                                 dimension_numbers=('NWC', 'WIO', 'NWC'),
                                 feature_group_count=D_CONV)
    return y + b, full[:, -CONV_HIST:]


def trunk_layer(l, x, pos, past_k, past_v, past_conv, p):
    B, T, _ = x.shape
    lam_init = lambda_init(l)
    proj = jnp.einsum('btd,de->bte', x, p['w_in'][l])
    q, k, v, ca, cg = jnp.split(proj, [D_ATTN, 2 * D_ATTN, 3 * D_ATTN, 3 * D_ATTN + D_CONV], axis=-1)
    q = rope(q.reshape(B, T, N_SUB_HEADS, HEAD_DIM), pos)
    k = rope(k.reshape(B, T, N_SUB_HEADS, HEAD_DIM), pos)
    v = v.reshape(B, T, N_DIFF_HEADS, V_DIM)
    lam = (jnp.exp(jnp.sum(p['lambda_q1'][l].astype(jnp.float32) * p['lambda_k1'][l].astype(jnp.float32)))
           - jnp.exp(jnp.sum(p['lambda_q2'][l].astype(jnp.float32) * p['lambda_k2'][l].astype(jnp.float32)))
           + lam_init)
    sub_g = p['subln_g'][l].astype(jnp.float32)
    if past_k is None:
        attn = prompt_diff_attention(q, k, v, lam, sub_g, lam_init)
        hist = jnp.zeros((B, CONV_HIST, D_CONV), x.dtype)
    else:
        k_all = jnp.concatenate([past_k, k], axis=1)
        v_all = jnp.concatenate([past_v, v], axis=1)
        attn = diff_core(q, k_all, v_all, None, lam, sub_g, lam_init)
        hist = past_conv
    u = ca * jax.nn.sigmoid(cg)
    c, conv_state = causal_depthwise_conv(u, hist, p['conv_w'][l], p['conv_b'][l])
    c = jax.nn.silu(layer_norm(c, p['conv_ln_g'][l], p['conv_ln_b'][l]))
    mix = jnp.concatenate([attn.reshape(B, T, D_ATTN), c], axis=-1)
    y = jnp.einsum('bte,ed->btd', mix, p['w_out'][l])
    x = layer_norm(DEEPNORM_ALPHA * x + y, p['ln1_g'][l], p['ln1_b'][l])
    h = jnp.square(jax.nn.relu(jnp.einsum('btd,df->btf', x, p['w_ff1'][l])))
    x = layer_norm(DEEPNORM_ALPHA * x + jnp.einsum('btf,fd->btd', h, p['w_ff2'][l]), p['ln2_g'][l], p['ln2_b'][l])
    return x, k, v, conv_state


def setup_inputs(seed: int = 0) -> dict:
    key = jax.random.key(seed)
    ks = jax.random.split(key, 24)
    f32 = jnp.float32
    nrm = lambda k, shape, s: jax.random.normal(k, shape, f32) * s
    w_in = nrm(ks[0], (DEPTH, D_MODEL, D_IN_PROJ), D_MODEL ** -0.5)
    w_in = w_in.at[..., 2 * D_ATTN:3 * D_ATTN].multiply(DEEPNORM_BETA)
    return {
        'x_prompt': nrm(ks[1], (BATCH, SEQ, D_MODEL), 1.0),
        'x_sample': nrm(ks[2], (DEC_BATCH, DEC_SEQ, D_MODEL), 1.0),
        'cache_k': nrm(ks[3], (DEPTH, DEC_BATCH, PAST_LEN, N_SUB_HEADS, HEAD_DIM), 1.0),
        'cache_v': nrm(ks[4], (DEPTH, DEC_BATCH, PAST_LEN, N_DIFF_HEADS, V_DIM), DEEPNORM_BETA),
        'cache_conv': nrm(ks[5], (DEPTH, DEC_BATCH, CONV_HIST, D_CONV), 0.5),
        'w_in': w_in,
        'lambda_q1': nrm(ks[6], (DEPTH, HEAD_DIM), 0.1),
        'lambda_k1': nrm(ks[7], (DEPTH, HEAD_DIM), 0.1),
        'lambda_q2': nrm(ks[8], (DEPTH, HEAD_DIM), 0.1),
        'lambda_k2': nrm(ks[9], (DEPTH, HEAD_DIM), 0.1),
        'subln_g': 1.0 + nrm(ks[10], (DEPTH, V_DIM), 0.02),
        'conv_w': nrm(ks[11], (DEPTH, CONV_WIDTH, D_CONV), CONV_WIDTH ** -0.5),
        'conv_b': nrm(ks[12], (DEPTH, D_CONV), 0.02),
        'conv_ln_g': 1.0 + nrm(ks[13], (DEPTH, D_CONV), 0.02),
        'conv_ln_b': nrm(ks[14], (DEPTH, D_CONV), 0.02),
        'w_out': nrm(ks[15], (DEPTH, D_MIX, D_MODEL), DEEPNORM_BETA * D_MIX ** -0.5),
        'ln1_g': 1.0 + nrm(ks[16], (DEPTH, D_MODEL), 0.02),
        'ln1_b': nrm(ks[17], (DEPTH, D_MODEL), 0.02),
        'w_ff1': nrm(ks[18], (DEPTH, D_MODEL, D_FF), DEEPNORM_BETA * D_MODEL ** -0.5),
        'w_ff2': nrm(ks[19], (DEPTH, D_FF, D_MODEL), DEEPNORM_BETA * D_FF ** -0.5),
        'ln2_g': 1.0 + nrm(ks[20], (DEPTH, D_MODEL), 0.02),
        'ln2_b': nrm(ks[21], (DEPTH, D_MODEL), 0.02),
    }


def reference(x_prompt, x_sample, cache_k, cache_v, cache_conv, w_in, lambda_q1, lambda_k1,
              lambda_q2, lambda_k2, subln_g, conv_w, conv_b, conv_ln_g, conv_ln_b, w_out,
              ln1_g, ln1_b, w_ff1, w_ff2, ln2_g, ln2_b):
    p = {'w_in': w_in, 'lambda_q1': lambda_q1, 'lambda_k1': lambda_k1, 'lambda_q2': lambda_q2,
         'lambda_k2': lambda_k2, 'subln_g': subln_g, 'conv_w': conv_w, 'conv_b': conv_b,
         'conv_ln_g': conv_ln_g, 'conv_ln_b': conv_ln_b, 'w_out': w_out, 'ln1_g': ln1_g,
         'ln1_b': ln1_b, 'w_ff1': w_ff1, 'w_ff2': w_ff2, 'ln2_g': ln2_g, 'ln2_b': ln2_b}
    T_p = x_prompt.shape[1]
    T_s = x_sample.shape[1]
    P = cache_k.shape[2]
    pos_p = jnp.arange(T_p)
    pos_s = P + jnp.arange(T_s)
    xp, xs = x_prompt, x_sample
    kp, vp, cp, ksl, vsl, csl = [], [], [], [], [], []
    for l in range(DEPTH):
        xp, k_new, v_new, c_new = trunk_layer(l, xp, pos_p, None, None, None, p)
        kp.append(k_new); vp.append(v_new); cp.append(c_new)
        xs, k_new, v_new, c_new = trunk_layer(l, xs, pos_s, cache_k[l], cache_v[l], cache_conv[l], p)
        ksl.append(k_new); vsl.append(v_new); csl.append(c_new)
    new_k_prompt = jnp.stack(kp)
    new_v_prompt = jnp.stack(vp)
    new_conv_prompt = jnp.stack(cp)
    new_k_sample = jnp.stack(ksl)
    new_v_sample = jnp.stack(vsl)
    new_conv_sample = jnp.stack(csl)
    return (xp, xs, new_k_prompt, new_v_prompt, new_conv_prompt, new_k_sample, new_v_sample, new_conv_sample)
```

```python
import functools
import math

import jax
import jax.numpy as jnp
from jax import lax
from jax.experimental import pallas as pl
from jax.experimental.pallas import tpu as pltpu

D_MODEL = 1024
D_ATTN = 512
D_CONV = 512
N_DIFF_HEADS = 4
HEAD_DIM = 64
V_DIM = 128
CHUNK = 64
CONV_WIDTH = 31
CONV_HIST = CONV_WIDTH - 1
HALO = 32
D_FF = 4096
FF_CHUNK = 1024
ROPE_THETA = 10000.0
LN_EPS = 1e-5
RMS_EPS = 1e-5
DEPTH = 2
DEEPNORM_ALPHA = (2.0 * DEPTH) ** 0.25
NEG_INF = -1e30
VMEM_LIMIT_BYTES = 56 * 1024 * 1024

BF16 = jnp.bfloat16
F32 = jnp.float32


def _lambda_init(layer):
    return 0.8 - 0.6 * math.exp(-0.3 * layer)


def _params(n_axes):
    return pltpu.CompilerParams(dimension_semantics=("arbitrary",) * n_axes,
                                vmem_limit_bytes=VMEM_LIMIT_BYTES)


def _const_spec(shape):
    return pl.BlockSpec(shape, lambda *_: (0,) * len(shape), pipeline_mode=pl.Buffered(1))


def _layer_norm(z, g, b):
    mu = jnp.mean(z, axis=-1, keepdims=True)
    d = z - mu
    var = jnp.mean(d * d, axis=-1, keepdims=True)
    return d * lax.rsqrt(var + LN_EPS) * g + b


def _in_proj_kernel(x_ref, w_ref, cos_ref, sin_ref, q_ref, kf_ref, kb_ref, vf_ref, vb_ref, u_ref):
    tm = x_ref.shape[0]
    xb = x_ref[...].astype(BF16)
    cos = jnp.concatenate([cos_ref[...]] * (D_ATTN // 128), axis=1)
    sin = jnp.concatenate([sin_ref[...]] * (D_ATTN // 128), axis=1)
    lane = lax.broadcasted_iota(jnp.int32, (tm, D_ATTN), 1)
    first_half = (lane % HEAD_DIM) < (HEAD_DIM // 2)

    def proj(c0, width):
        return jnp.dot(xb, w_ref[:, c0:c0 + width], preferred_element_type=F32)

    def rope(p):
        partner = jnp.where(first_half,
                            pltpu.roll(p, D_ATTN - HEAD_DIM // 2, 1),
                            pltpu.roll(p, HEAD_DIM // 2, 1))
        return p * cos + partner * sin

    q = rope(proj(0, D_ATTN)) * (HEAD_DIM ** -0.5)
    q_ref[...] = q.astype(BF16)
    k = rope(proj(D_ATTN, D_ATTN))
    kf_ref[...] = k
    kb_ref[...] = k.astype(BF16)
    v = proj(2 * D_ATTN, D_ATTN)
    vf_ref[...] = v
    vb_ref[...] = v.astype(BF16)
    ca = proj(3 * D_ATTN, D_CONV)
    cg = proj(3 * D_ATTN + D_CONV, D_CONV)
    u_ref[...] = ca * jax.nn.sigmoid(cg)


def _in_proj(x, w_in, cos, sin, tm):
    n = x.shape[0]
    n_tab = cos.shape[0] // tm
    tok = lambda width: pl.BlockSpec((tm, width), lambda i: (i, 0))
    tab = pl.BlockSpec((tm, 128), lambda i: (i % n_tab, 0))
    return pl.pallas_call(
        _in_proj_kernel,
        grid=(n // tm,),
        in_specs=[tok(D_MODEL), _const_spec(w_in.shape), tab, tab],
        out_specs=[tok(D_ATTN)] * 5 + [tok(D_CONV)],
        out_shape=[jax.ShapeDtypeStruct((n, D_ATTN), BF16),
                   jax.ShapeDtypeStruct((n, D_ATTN), F32),
                   jax.ShapeDtypeStruct((n, D_ATTN), BF16),
                   jax.ShapeDtypeStruct((n, D_ATTN), F32),
                   jax.ShapeDtypeStruct((n, D_ATTN), BF16),
                   jax.ShapeDtypeStruct((n, D_CONV), F32)],
        compiler_params=_params(1),
        name="in_proj",
    )(x, w_in, cos, sin)


def _stack_sub_heads(q):
    lane = lax.broadcasted_iota(jnp.int32, q.shape, 1)
    zero = jnp.zeros_like(q)
    return jnp.concatenate([jnp.where(lane < HEAD_DIM, q, zero),
                            jnp.where(lane >= HEAD_DIM, q, zero)], axis=0)


def _lambda(lq1, lk1, lq2, lk2, lam_init):
    s1 = jnp.sum(lq1[...] * lk1[...], axis=1, keepdims=True)
    s2 = jnp.sum(lq2[...] * lk2[...], axis=1, keepdims=True)
    return jnp.exp(s1) - jnp.exp(s2) + lam_init


def _diff_finish(o_all, lam, g, lam_init):
    t = o_all.shape[0] // 2
    o = o_all[:t] - lam * o_all[t:]
    o = o * lax.rsqrt(jnp.mean(o * o, axis=-1, keepdims=True) + RMS_EPS)
    return o * g * (1.0 - lam_init)


def _qk(q, k):
    return lax.dot_general(q, k, (((1,), (1,)), ((), ())), preferred_element_type=F32)


def _attn_prompt_kernel(lq1, lk1, lq2, lk2, g_ref, q_ref, k_ref, v_ref, o_ref,
                        m_sc, l_sc, acc_sc, *, lam_init, tq):
    qi = pl.program_id(2)
    qq = _stack_sub_heads(q_ref[...])
    m_sc[...] = jnp.full(m_sc.shape, NEG_INF, F32)
    l_sc[...] = jnp.zeros(l_sc.shape, F32)
    acc_sc[...] = jnp.zeros(acc_sc.shape, F32)

    def step(kb, masked):
        start = pl.multiple_of(kb * tq, tq)
        k = k_ref[pl.ds(start, tq), :]
        v = v_ref[pl.ds(start, tq), :]
        s = _qk(qq, k)
        if masked:
            q_chunk = (lax.broadcasted_iota(jnp.int32, s.shape, 0) % tq) // CHUNK
            k_chunk = lax.broadcasted_iota(jnp.int32, s.shape, 1) // CHUNK
            s = jnp.where(k_chunk <= q_chunk, s, NEG_INF)
        m_prev = m_sc[...]
        m_new = jnp.maximum(m_prev, jnp.max(s, axis=1, keepdims=True))
        alpha = jnp.exp(m_prev - m_new)
        p = jnp.exp(s - m_new)
        l_sc[...] = alpha * l_sc[...] + jnp.sum(p, axis=1, keepdims=True)
        acc_sc[...] = alpha * acc_sc[...] + jnp.dot(p.astype(BF16), v, preferred_element_type=F32)
        m_sc[...] = m_new

    def body(kb, carry):
        step(kb, False)
        return carry

    lax.fori_loop(0, qi, body, 0)
    step(qi, True)

    lam = _lambda(lq1, lk1, lq2, lk2, lam_init)
    o_all = acc_sc[...] / l_sc[...]
    o_ref[...] = _diff_finish(o_all, lam, g_ref[...], lam_init).astype(o_ref.dtype)


def _attn_prompt(q, k, v, lam_params, g, lam_init, batch, seq, tq):
    nq = seq // tq
    small = pl.BlockSpec((1, HEAD_DIM), lambda b, h, i: (0, 0))
    kv_spec = pl.BlockSpec((seq, V_DIM), lambda b, h, i: (b, h))
    q_spec = pl.BlockSpec((tq, V_DIM), lambda b, h, i: (b * nq + i, h))
    return pl.pallas_call(
        functools.partial(_attn_prompt_kernel, lam_init=lam_init, tq=tq),
        grid=(batch, N_DIFF_HEADS, nq),
        in_specs=[small] * 4 + [pl.BlockSpec((1, V_DIM), lambda b, h, i: (0, 0)),
                                q_spec, kv_spec, kv_spec],
        out_specs=q_spec,
        out_shape=jax.ShapeDtypeStruct((batch * seq, D_ATTN), BF16),
        scratch_shapes=[pltpu.VMEM((2 * tq, 1), F32), pltpu.VMEM((2 * tq, 1), F32),
                        pltpu.VMEM((2 * tq, V_DIM), F32)],
        compiler_params=_params(3),
        name="attn_prompt",
    )(*lam_params, g, q, k, v)


def _attn_sample_kernel(lq1, lk1, lq2, lk2, g_ref, q_ref, kn_ref, vn_ref, ck_ref, cv_ref, o_ref,
                        *, lam_init):
    qq = _stack_sub_heads(q_ref[...])
    s_past = _qk(qq, ck_ref[...].astype(BF16))
    s_new = _qk(qq, kn_ref[...])
    m = jnp.maximum(jnp.max(s_past, axis=1, keepdims=True), jnp.max(s_new, axis=1, keepdims=True))
    p_past = jnp.exp(s_past - m)
    p_new = jnp.exp(s_new - m)
    l = jnp.sum(p_past, axis=1, keepdims=True) + jnp.sum(p_new, axis=1, keepdims=True)
    acc = (jnp.dot(p_past.astype(BF16), cv_ref[...].astype(BF16), preferred_element_type=F32)
           + jnp.dot(p_new.astype(BF16), vn_ref[...], preferred_element_type=F32))
    lam = _lambda(lq1, lk1, lq2, lk2, lam_init)
    o_ref[...] = _diff_finish(acc / l, lam, g_ref[...], lam_init).astype(o_ref.dtype)


def _attn_sample(q, k, v, cache_k, cache_v, lam_params, g, lam_init, batch, seq, past):
    small = pl.BlockSpec((1, HEAD_DIM), lambda b, h: (0, 0))
    new_spec = pl.BlockSpec((seq, V_DIM), lambda b, h: (b, h))
    past_spec = pl.BlockSpec((past, V_DIM), lambda b, h: (b, h))
    return pl.pallas_call(
        functools.partial(_attn_sample_kernel, lam_init=lam_init),
        grid=(batch, N_DIFF_HEADS),
        in_specs=[small] * 4 + [pl.BlockSpec((1, V_DIM), lambda b, h: (0, 0)),
                                new_spec, new_spec, new_spec, past_spec, past_spec],
        out_specs=new_spec,
        out_shape=jax.ShapeDtypeStruct((batch * seq, D_ATTN), BF16),
        compiler_params=_params(2),
        name="attn_sample",
    )(*lam_params, g, q, k, v, cache_k, cache_v)


CONV_ROWS = 32


def _mix_kernel(attn_ref, u_ref, halo_ref, x_ref, wo_ref, cw_ref, cb_ref, cg_ref, cbeta_ref,
                g1_ref, b1_ref, o_ref, win_sc, c_sc, *, tiles_per_seq):
    tm = u_ref.shape[0]
    halo = halo_ref[...]
    if tiles_per_seq is not None:
        keep = (pl.program_id(0) % tiles_per_seq) != 0
        halo = jnp.where(keep, halo, jnp.zeros_like(halo))
    win_sc[0:HALO, :] = halo
    win_sc[HALO:HALO + tm, :] = u_ref[...]

    for r in range(tm // CONV_ROWS):
        base = r * CONV_ROWS
        acc = jnp.zeros((CONV_ROWS, D_CONV), F32)
        for j in range(CONV_WIDTH):
            lo = base + (HALO - CONV_HIST) + j
            acc = acc + cw_ref[j:j + 1, :] * win_sc[lo:lo + CONV_ROWS, :]
        c = _layer_norm(acc + cb_ref[...], cg_ref[...], cbeta_ref[...])
        c = c * jax.nn.sigmoid(c)
        c_sc[base:base + CONV_ROWS, :] = c.astype(BF16)

    y = (jnp.dot(attn_ref[...], wo_ref[0:D_ATTN, :], preferred_element_type=F32)
         + jnp.dot(c_sc[...], wo_ref[D_ATTN:D_ATTN + D_CONV, :], preferred_element_type=F32))
    o_ref[...] = _layer_norm(DEEPNORM_ALPHA * x_ref[...] + y, g1_ref[...], b1_ref[...])


def _mix(attn, u, halo_src, x, w_out, conv_w, conv_b, conv_g, conv_beta, g1, b1, tm, tiles_per_seq):
    n = x.shape[0]
    tok = lambda width: pl.BlockSpec((tm, width), lambda i: (i, 0))
    if tiles_per_seq is None:
        halo_spec = pl.BlockSpec((HALO, D_CONV), lambda i: (i, 0))
    else:
        halo_spec = pl.BlockSpec((HALO, D_CONV), lambda i: (jnp.maximum(i * (tm // HALO) - 1, 0), 0))
    vec = lambda width: _const_spec((1, width))
    return pl.pallas_call(
        functools.partial(_mix_kernel, tiles_per_seq=tiles_per_seq),
        grid=(n // tm,),
        in_specs=[tok(D_ATTN), tok(D_CONV), halo_spec, tok(D_MODEL), _const_spec(w_out.shape),
                  _const_spec(conv_w.shape), vec(D_CONV), vec(D_CONV), vec(D_CONV),
                  vec(D_MODEL), vec(D_MODEL)],
        out_specs=tok(D_MODEL),
        out_shape=jax.ShapeDtypeStruct((n, D_MODEL), F32),
        scratch_shapes=[pltpu.VMEM((HALO + tm, D_CONV), F32), pltpu.VMEM((tm, D_CONV), BF16)],
        compiler_params=_params(1),
        name="mix",
    )(attn, u, halo_src, x, w_out, conv_w, conv_b, conv_g, conv_beta, g1, b1)


def _ffn_kernel(x_ref, w1_ref, w2_ref, g_ref, b_ref, o_ref):
    x = x_ref[...]
    xb = x.astype(BF16)
    acc = jnp.zeros(x.shape, F32)
    for c in range(D_FF // FF_CHUNK):
        h = jnp.dot(xb, w1_ref[:, c * FF_CHUNK:(c + 1) * FF_CHUNK], preferred_element_type=F32)
        h = jnp.maximum(h, 0.0)
        h = (h * h).astype(BF16)
        acc = acc + jnp.dot(h, w2_ref[c * FF_CHUNK:(c + 1) * FF_CHUNK, :], preferred_element_type=F32)
    o_ref[...] = _layer_norm(DEEPNORM_ALPHA * x + acc, g_ref[...], b_ref[...])


def _ffn(x, w1, w2, g, b, tm):
    n = x.shape[0]
    tok = pl.BlockSpec((tm, D_MODEL), lambda i: (i, 0))
    return pl.pallas_call(
        _ffn_kernel,
        grid=(n // tm,),
        in_specs=[tok, _const_spec(w1.shape), _const_spec(w2.shape),
                  _const_spec((1, D_MODEL)), _const_spec((1, D_MODEL))],
        out_specs=tok,
        out_shape=jax.ShapeDtypeStruct((n, D_MODEL), F32),
        compiler_params=_params(1),
        name="ffn",
    )(x, w1, w2, g, b)


def _rope_tables(pos):
    half = HEAD_DIM // 2
    inv = ROPE_THETA ** (-jnp.arange(half, dtype=F32) / half)
    ang = pos.astype(F32)[:, None] * inv[None, :]
    cos = jnp.cos(ang)
    sin = jnp.sin(ang)
    cos = jnp.concatenate([cos, cos, cos, cos], axis=1)
    sin = jnp.concatenate([-sin, sin, -sin, sin], axis=1)
    return cos, sin


def _row(a):
    return a.reshape(1, -1)


TM = 512
TQ = 256


def kernel(x_prompt, x_sample, cache_k, cache_v, cache_conv, w_in, lambda_q1, lambda_k1, lambda_q2,
           lambda_k2, subln_g, conv_w, conv_b, conv_ln_g, conv_ln_b, w_out, ln1_g, ln1_b, w_ff1,
           w_ff2, ln2_g, ln2_b):
    B, T, _ = x_prompt.shape
    Bs, Ts, _ = x_sample.shape
    P = cache_k.shape[2]
    depth = w_in.shape[0]

    w_in_b = w_in.astype(BF16)
    w_out_b = w_out.astype(BF16)
    w_ff1_b = w_ff1.astype(BF16)
    w_ff2_b = w_ff2.astype(BF16)

    cos_p, sin_p = _rope_tables(jnp.arange(T))
    cos_s, sin_s = _rope_tables(P + jnp.arange(Ts))
    reps = TM // Ts
    cos_s = jnp.tile(cos_s, (reps, 1))
    sin_s = jnp.tile(sin_s, (reps, 1))

    xp = x_prompt.reshape(B * T, D_MODEL)
    xs = x_sample.reshape(Bs * Ts, D_MODEL)
    hist_s = jnp.pad(cache_conv, ((0, 0), (0, 0), (HALO - CONV_HIST, 0), (0, 0)))

    outs = {k: [] for k in ("kp", "vp", "cp", "ks", "vs", "cs")}
    for l in range(depth):
        lam_init = _lambda_init(l)
        lam_params = [_row(lambda_q1[l]), _row(lambda_k1[l]), _row(lambda_q2[l]), _row(lambda_k2[l])]
        g = _row(subln_g[l])
        cw, cb = conv_w[l], _row(conv_b[l])
        cg, cbeta = _row(conv_ln_g[l]), _row(conv_ln_b[l])
        g1, b1 = _row(ln1_g[l]), _row(ln1_b[l])
        g2, b2 = _row(ln2_g[l]), _row(ln2_b[l])

        q, kf, kb, vf, vb, u = _in_proj(xp, w_in_b[l], cos_p, sin_p, TM)
        attn = _attn_prompt(q, kb, vb, lam_params, g, lam_init, B, T, TQ)
        xp = _mix(attn, u, u, xp, w_out_b[l], cw, cb, cg, cbeta, g1, b1, TM, T // TM)
        xp = _ffn(xp, w_ff1_b[l], w_ff2_b[l], g2, b2, TM)
        outs["kp"].append(kf.reshape(B, T, 2 * N_DIFF_HEADS, HEAD_DIM))
        outs["vp"].append(vf.reshape(B, T, N_DIFF_HEADS, V_DIM))
        outs["cp"].append(u.reshape(B, T, D_CONV)[:, T - CONV_HIST:])

        q, kf, kb, vf, vb, u = _in_proj(xs, w_in_b[l], cos_s, sin_s, TM)
        attn = _attn_sample(q, kb, vb, cache_k[l].reshape(Bs * P, D_ATTN),
                            cache_v[l].reshape(Bs * P, D_ATTN), lam_params, g, lam_init, Bs, Ts, P)
        xs = _mix(attn, u, hist_s[l].reshape(Bs * HALO, D_CONV), xs, w_out_b[l], cw, cb, cg, cbeta,
                  g1, b1, Ts, None)
        xs = _ffn(xs, w_ff1_b[l], w_ff2_b[l], g2, b2, TM)
        outs["ks"].append(kf.reshape(Bs, Ts, 2 * N_DIFF_HEADS, HEAD_DIM))
        outs["vs"].append(vf.reshape(Bs, Ts, N_DIFF_HEADS, V_DIM))
        outs["cs"].append(u.reshape(Bs, Ts, D_CONV)[:, Ts - CONV_HIST:])

    return (xp.reshape(B, T, D_MODEL), xs.reshape(Bs, Ts, D_MODEL),
            jnp.stack(outs["kp"]), jnp.stack(outs["vp"]), jnp.stack(outs["cp"]),
            jnp.stack(outs["ks"]), jnp.stack(outs["vs"]), jnp.stack(outs["cs"]))
```

```python
import functools
import math

import jax
import jax.numpy as jnp
from jax import lax
from jax.experimental import pallas as pl
from jax.experimental.pallas import tpu as pltpu

D_MODEL = 1024
D_ATTN = 512
D_CONV = 512
N_DIFF_HEADS = 4
N_SUB_HEADS = 8
HEAD_DIM = 64
HALF = HEAD_DIM // 2
V_DIM = 128
LANES = 128
CHUNK = 64
CONV_WIDTH = 31
CONV_HIST = CONV_WIDTH - 1
HALO = 32
D_FF = 4096
FF_CHUNK = 1024
ROPE_THETA = 10000.0
LN_EPS = 1e-5
RMS_EPS = 1e-5
DEPTH = 2
DEEPNORM_ALPHA = (2.0 * DEPTH) ** 0.25
NEG_INF = -1e30
LOG2E = math.log2(math.e)
Q_SCALE = HEAD_DIM ** -0.5 * LOG2E
VMEM_LIMIT_BYTES = 56 * 1024 * 1024

BF16 = jnp.bfloat16
F32 = jnp.float32


def _lambda_init(layer):
    return 0.8 - 0.6 * math.exp(-0.3 * layer)


def _params(n_axes):
    return pltpu.CompilerParams(dimension_semantics=("arbitrary",) * n_axes,
                                vmem_limit_bytes=VMEM_LIMIT_BYTES)


def _const_spec(shape):
    return pl.BlockSpec(shape, lambda *_: (0,) * len(shape), pipeline_mode=pl.Buffered(1))


def _layer_norm(z, g, b):
    mu = jnp.mean(z, axis=-1, keepdims=True)
    d = z - mu
    var = jnp.mean(d * d, axis=-1, keepdims=True)
    return d * lax.rsqrt(var + LN_EPS) * g + b


def _dot_nt(a, b):
    return lax.dot_general(a, b, (((1,), (1,)), ((), ())), preferred_element_type=F32)


def _rope_rows(p, cos_ref, sin_ref):
    n = p.shape[1] // LANES
    cos = jnp.concatenate([cos_ref[...]] * n, axis=1)
    sin = jnp.concatenate([sin_ref[...]] * n, axis=1)
    lane = lax.broadcasted_iota(jnp.int32, p.shape, 1)
    partner = jnp.where((lane % HEAD_DIM) < HALF,
                        pltpu.roll(p, p.shape[1] - HALF, 1),
                        pltpu.roll(p, HALF, 1))
    return p * cos + partner * sin


def _rope_cols(pt, cos_ref, sin_ref):
    cos = cos_ref[...]
    sin = sin_ref[...]
    parts = []
    for s in range(pt.shape[0] // HEAD_DIM):
        a = pt[s * HEAD_DIM:s * HEAD_DIM + HALF]
        b = pt[s * HEAD_DIM + HALF:(s + 1) * HEAD_DIM]
        parts += [a * cos - b * sin, b * cos + a * sin]
    return jnp.concatenate(parts, axis=0)


def _in_proj_common(x_ref, w_ref, cos_ref, sin_ref, q_ref, vb_ref, u_ref):
    xb = x_ref[...].astype(BF16)

    def proj(c0, width):
        return jnp.dot(xb, w_ref[:, c0:c0 + width], preferred_element_type=F32)

    q_ref[...] = (_rope_rows(proj(0, D_ATTN), cos_ref, sin_ref) * Q_SCALE).astype(BF16)
    v = proj(2 * D_ATTN, D_ATTN)
    vb_ref[...] = v.astype(BF16)
    ca = proj(3 * D_ATTN, D_CONV)
    cg = proj(3 * D_ATTN + D_CONV, D_CONV)
    u_ref[...] = ca * jax.nn.sigmoid(cg)
    return xb, v, proj


def _in_proj_rows_kernel(x_ref, w_ref, cos_ref, sin_ref, q_ref, kf_ref, kb_ref, vf_ref, vb_ref, u_ref):
    _, v, proj = _in_proj_common(x_ref, w_ref, cos_ref, sin_ref, q_ref, vb_ref, u_ref)
    vf_ref[...] = v
    k = _rope_rows(proj(D_ATTN, D_ATTN), cos_ref, sin_ref)
    kf_ref[...] = k
    kb_ref[...] = k.astype(BF16)


def _in_proj_cols_kernel(x_ref, w_ref, wkt_ref, cos_ref, sin_ref, cost_ref, sint_ref, *rest):
    q_ref, ktf_ref, ktb_ref, vf_ref, vb_ref, u_ref = rest[-6:]
    tm = x_ref.shape[0]
    xb, v, _ = _in_proj_common(x_ref, w_ref, cos_ref, sin_ref, q_ref, vb_ref, u_ref)
    kt = _rope_cols(_dot_nt(wkt_ref[...], xb), cost_ref, sint_ref)
    ktf_ref[...] = kt
    ktb_ref[...] = kt.astype(BF16)
    for h in range(N_DIFF_HEADS):
        vf_ref[pl.ds(h, tm, stride=N_DIFF_HEADS), :] = v[:, h * V_DIM:(h + 1) * V_DIM]


def _in_proj_rows(x, w_in, cos, sin, tm):
    n = x.shape[0]
    n_tab = cos.shape[0] // tm
    tok = lambda width: pl.BlockSpec((tm, width), lambda i: (i, 0))
    tab = pl.BlockSpec((tm, LANES), lambda i: (i % n_tab, 0))
    return pl.pallas_call(
        _in_proj_rows_kernel,
        grid=(n // tm,),
        in_specs=[tok(D_MODEL), _const_spec(w_in.shape), tab, tab],
        out_specs=[tok(D_ATTN)] * 5 + [tok(D_CONV)],
        out_shape=[jax.ShapeDtypeStruct((n, D_ATTN), BF16),
                   jax.ShapeDtypeStruct((n, D_ATTN), F32),
                   jax.ShapeDtypeStruct((n, D_ATTN), BF16),
                   jax.ShapeDtypeStruct((n, D_ATTN), F32),
                   jax.ShapeDtypeStruct((n, D_ATTN), BF16),
                   jax.ShapeDtypeStruct((n, D_CONV), F32)],
        compiler_params=_params(1),
        name="in_proj_rows",
    )(x, w_in, cos, sin)


def _in_proj_cols(x, w_in, wk_t, cos, sin, cos_t, sin_t, layer, depth, batch, seq, tm, prev):
    n = x.shape[0]
    nt = seq // tm
    tok = lambda width: pl.BlockSpec((tm, width), lambda i: (i, 0))
    tab = pl.BlockSpec((tm, LANES), lambda i: (i % nt, 0))
    tab_t = pl.BlockSpec((HALF, tm), lambda i: (0, i % nt))
    in_specs = [tok(D_MODEL), _const_spec(w_in.shape), _const_spec(wk_t.shape), tab, tab, tab_t, tab_t]
    args = [x, w_in, wk_t, cos, sin, cos_t, sin_t]
    aliases = {}
    if prev is not None:
        in_specs += [pl.BlockSpec(memory_space=pl.ANY)] * 2
        aliases = {len(args): 1, len(args) + 1: 3}
        args += list(prev)
    out_specs = [tok(D_ATTN),
                 pl.BlockSpec((None, None, D_ATTN, tm), lambda i: (layer, i // nt, 0, i % nt)),
                 pl.BlockSpec((None, D_ATTN, tm), lambda i: (i // nt, 0, i % nt)),
                 pl.BlockSpec((None, tm * N_DIFF_HEADS, V_DIM), lambda i: (layer, i, 0)),
                 tok(D_ATTN), tok(D_CONV)]
    out_shape = [jax.ShapeDtypeStruct((n, D_ATTN), BF16),
                 jax.ShapeDtypeStruct((depth, batch, D_ATTN, seq), F32),
                 jax.ShapeDtypeStruct((batch, D_ATTN, seq), BF16),
                 jax.ShapeDtypeStruct((depth, n * N_DIFF_HEADS, V_DIM), F32),
                 jax.ShapeDtypeStruct((n, D_ATTN), BF16),
                 jax.ShapeDtypeStruct((n, D_CONV), F32)]
    return pl.pallas_call(
        _in_proj_cols_kernel,
        grid=(n // tm,),
        in_specs=in_specs,
        out_specs=out_specs,
        out_shape=out_shape,
        input_output_aliases=aliases,
        compiler_params=_params(1),
        name="in_proj_cols",
    )(*args)


def _stack_sub_heads(q):
    lane = lax.broadcasted_iota(jnp.int32, q.shape, 1)
    zero = jnp.zeros_like(q)
    return jnp.concatenate([jnp.where(lane < HEAD_DIM, q, zero),
                            jnp.where(lane >= HEAD_DIM, q, zero)], axis=0)


def _lambda(lq1, lk1, lq2, lk2, lam_init):
    s1 = jnp.sum(lq1[...] * lk1[...], axis=1, keepdims=True)
    s2 = jnp.sum(lq2[...] * lk2[...], axis=1, keepdims=True)
    return jnp.exp(s1) - jnp.exp(s2) + lam_init


def _diff_finish(o_all, lam, g, lam_init):
    t = o_all.shape[0] // 2
    o = o_all[:t] - lam * o_all[t:]
    o = o * lax.rsqrt(jnp.mean(o * o, axis=-1, keepdims=True) + RMS_EPS)
    return o * g * (1.0 - lam_init)


def _row_max(s):
    m = s[:, 0:LANES]
    for c in range(1, s.shape[1] // LANES):
        m = jnp.maximum(m, s[:, c * LANES:(c + 1) * LANES])
    return jnp.broadcast_to(jnp.max(m, axis=1, keepdims=True), m.shape)


def _with_ones(v):
    return jnp.concatenate([v, jnp.ones(v.shape, v.dtype)], axis=1)


def _wide(a, width):
    return jnp.concatenate([a] * (width // LANES), axis=1)


def _attn_prompt_kernel(lq1, lk1, lq2, lk2, g_ref, q_ref, kt_ref, v_ref, o_ref,
                        m_sc, acc_sc, *, lam_init, tq):
    qi = pl.program_id(2)
    qq = _stack_sub_heads(q_ref[...])
    m_sc[...] = jnp.full(m_sc.shape, NEG_INF, F32)
    acc_sc[...] = jnp.zeros(acc_sc.shape, F32)

    def step(kb, masked):
        start = pl.multiple_of(kb * tq, tq)
        kt = kt_ref[:, pl.ds(start, tq)]
        v = v_ref[pl.ds(start, tq), :]
        s = jnp.dot(qq, kt, preferred_element_type=F32)
        if masked:
            q_chunk = (lax.broadcasted_iota(jnp.int32, s.shape, 0) % tq) // CHUNK
            k_chunk = lax.broadcasted_iota(jnp.int32, s.shape, 1) // CHUNK
            s = jnp.where(k_chunk <= q_chunk, s, NEG_INF)
        m_prev = m_sc[...]
        m_new = jnp.maximum(m_prev, _row_max(s))
        alpha = jnp.exp2(m_prev - m_new)
        p = jnp.exp2(s - _wide(m_new, tq)).astype(BF16)
        pv = jnp.dot(p, _with_ones(v), preferred_element_type=F32)
        acc_sc[...] = _wide(alpha, 2 * V_DIM) * acc_sc[...] + pv
        m_sc[...] = m_new

    def body(kb, carry):
        step(kb, False)
        return carry

    lax.fori_loop(0, qi, body, 0)
    step(qi, True)

    lam = _lambda(lq1, lk1, lq2, lk2, lam_init)
    acc = acc_sc[...]
    o_all = acc[:, :V_DIM] / acc[:, V_DIM:]
    o_ref[...] = _diff_finish(o_all, lam, g_ref[...], lam_init).astype(o_ref.dtype)


def _attn_prompt(q, kt, v, lam_params, g, lam_init, batch, seq, tq):
    nq = seq // tq
    small = pl.BlockSpec((1, HEAD_DIM), lambda b, h, i: (0, 0))
    kt_spec = pl.BlockSpec((None, V_DIM, seq), lambda b, h, i: (b, h, 0))
    v_spec = pl.BlockSpec((seq, V_DIM), lambda b, h, i: (b, h))
    q_spec = pl.BlockSpec((tq, V_DIM), lambda b, h, i: (b * nq + i, h))
    return pl.pallas_call(
        functools.partial(_attn_prompt_kernel, lam_init=lam_init, tq=tq),
        grid=(batch, N_DIFF_HEADS, nq),
        in_specs=[small] * 4 + [pl.BlockSpec((1, V_DIM), lambda b, h, i: (0, 0)),
                                q_spec, kt_spec, v_spec],
        out_specs=q_spec,
        out_shape=jax.ShapeDtypeStruct((batch * seq, D_ATTN), BF16),
        scratch_shapes=[pltpu.VMEM((2 * tq, LANES), F32), pltpu.VMEM((2 * tq, 2 * V_DIM), F32)],
        compiler_params=_params(3),
        name="attn_prompt",
    )(*lam_params, g, q, kt, v)


def _attn_sample_kernel(lq1, lk1, lq2, lk2, g_ref, q_ref, kn_ref, vn_ref, ckt_ref, cv_ref, o_ref,
                        *, lam_init, past):
    lam = _lambda(lq1, lk1, lq2, lk2, lam_init)
    for h in range(N_DIFF_HEADS):
        cols = slice(h * V_DIM, (h + 1) * V_DIM)
        qq = _stack_sub_heads(q_ref[:, cols])
        s_past = jnp.dot(qq, ckt_ref[cols, :].astype(BF16), preferred_element_type=F32)
        s_new = _dot_nt(qq, kn_ref[:, cols])
        m = jnp.maximum(_row_max(s_past), jnp.max(s_new, axis=1, keepdims=True))
        p_past = jnp.exp2(s_past - _wide(m, past)).astype(BF16)
        p_new = jnp.exp2(s_new - m[:, :s_new.shape[1]]).astype(BF16)
        v_past = cv_ref[pl.ds(h, past, stride=N_DIFF_HEADS), :].astype(BF16)
        acc = (jnp.dot(p_past, _with_ones(v_past), preferred_element_type=F32)
               + jnp.dot(p_new, _with_ones(vn_ref[:, cols]), preferred_element_type=F32))
        o_all = acc[:, :V_DIM] / acc[:, V_DIM:]
        o_ref[:, cols] = _diff_finish(o_all, lam, g_ref[...], lam_init).astype(o_ref.dtype)


def _attn_sample(q, k, v, cache_kt, cache_v, lam_params, g, lam_init, layer, batch, seq, past):
    small = pl.BlockSpec((1, HEAD_DIM), lambda b: (0, 0))
    new_spec = pl.BlockSpec((seq, D_ATTN), lambda b: (b, 0))
    return pl.pallas_call(
        functools.partial(_attn_sample_kernel, lam_init=lam_init, past=past),
        grid=(batch,),
        in_specs=[small] * 4 + [pl.BlockSpec((1, V_DIM), lambda b: (0, 0)),
                                new_spec, new_spec, new_spec,
                                pl.BlockSpec((None, None, D_ATTN, past), lambda b: (layer, b, 0, 0)),
                                pl.BlockSpec((None, None, past * N_DIFF_HEADS, V_DIM),
                                             lambda b: (layer, b, 0, 0))],
        out_specs=new_spec,
        out_shape=jax.ShapeDtypeStruct((batch * seq, D_ATTN), BF16),
        compiler_params=_params(1),
        name="attn_sample",
    )(*lam_params, g, q, k, v, cache_kt, cache_v)


CONV_ROWS = 32


def _mix_kernel(attn_ref, u_ref, halo_ref, x_ref, wo_ref, cw_ref, cb_ref, cg_ref, cbeta_ref,
                g1_ref, b1_ref, o_ref, win_sc, c_sc, *, tiles_per_seq):
    tm = u_ref.shape[0]
    halo = halo_ref[...]
    if tiles_per_seq is not None:
        keep = (pl.program_id(0) % tiles_per_seq) != 0
        halo = jnp.where(keep, halo, jnp.zeros_like(halo))
    win_sc[0:HALO, :] = halo
    win_sc[HALO:HALO + tm, :] = u_ref[...]

    for r in range(tm // CONV_ROWS):
        base = r * CONV_ROWS
        acc = jnp.zeros((CONV_ROWS, D_CONV), F32)
        for j in range(CONV_WIDTH):
            lo = base + (HALO - CONV_HIST) + j
            acc = acc + cw_ref[j:j + 1, :] * win_sc[lo:lo + CONV_ROWS, :]
        c = _layer_norm(acc + cb_ref[...], cg_ref[...], cbeta_ref[...])
        c = c * jax.nn.sigmoid(c)
        c_sc[base:base + CONV_ROWS, :] = c.astype(BF16)

    y = (jnp.dot(attn_ref[...], wo_ref[0:D_ATTN, :], preferred_element_type=F32)
         + jnp.dot(c_sc[...], wo_ref[D_ATTN:D_ATTN + D_CONV, :], preferred_element_type=F32))
    o_ref[...] = _layer_norm(DEEPNORM_ALPHA * x_ref[...] + y, g1_ref[...], b1_ref[...])


def _mix(attn, u, halo_src, x, w_out, conv_w, conv_b, conv_g, conv_beta, g1, b1, tm, tiles_per_seq):
    n = x.shape[0]
    tok = lambda width: pl.BlockSpec((tm, width), lambda i: (i, 0))
    if tiles_per_seq is None:
        halo_spec = pl.BlockSpec((HALO, D_CONV), lambda i: (i, 0))
    else:
        halo_spec = pl.BlockSpec((HALO, D_CONV), lambda i: (jnp.maximum(i * (tm // HALO) - 1, 0), 0))
    vec = lambda width: _const_spec((1, width))
    return pl.pallas_call(
        functools.partial(_mix_kernel, tiles_per_seq=tiles_per_seq),
        grid=(n // tm,),
        in_specs=[tok(D_ATTN), tok(D_CONV), halo_spec, tok(D_MODEL), _const_spec(w_out.shape),
                  _const_spec(conv_w.shape), vec(D_CONV), vec(D_CONV), vec(D_CONV),
                  vec(D_MODEL), vec(D_MODEL)],
        out_specs=tok(D_MODEL),
        out_shape=jax.ShapeDtypeStruct((n, D_MODEL), F32),
        scratch_shapes=[pltpu.VMEM((HALO + tm, D_CONV), F32), pltpu.VMEM((tm, D_CONV), BF16)],
        compiler_params=_params(1),
        name="mix",
    )(attn, u, halo_src, x, w_out, conv_w, conv_b, conv_g, conv_beta, g1, b1)


def _ffn_kernel(x_ref, w1_ref, w2_ref, g_ref, b_ref, o_ref):
    x = x_ref[...]
    xb = x.astype(BF16)
    acc = jnp.zeros(x.shape, F32)
    for c in range(D_FF // FF_CHUNK):
        h = jnp.dot(xb, w1_ref[:, c * FF_CHUNK:(c + 1) * FF_CHUNK], preferred_element_type=F32)
        h = jnp.maximum(h, 0.0)
        h = (h * h).astype(BF16)
        acc = acc + jnp.dot(h, w2_ref[c * FF_CHUNK:(c + 1) * FF_CHUNK, :], preferred_element_type=F32)
    o_ref[...] = _layer_norm(DEEPNORM_ALPHA * x + acc, g_ref[...], b_ref[...])


def _ffn(x, w1, w2, g, b, tm):
    n = x.shape[0]
    tok = pl.BlockSpec((tm, D_MODEL), lambda i: (i, 0))
    return pl.pallas_call(
        _ffn_kernel,
        grid=(n // tm,),
        in_specs=[tok, _const_spec(w1.shape), _const_spec(w2.shape),
                  _const_spec((1, D_MODEL)), _const_spec((1, D_MODEL))],
        out_specs=tok,
        out_shape=jax.ShapeDtypeStruct((n, D_MODEL), F32),
        compiler_params=_params(1),
        name="ffn",
    )(x, w1, w2, g, b)


def _rope_angles(pos):
    inv = ROPE_THETA ** (-jnp.arange(HALF, dtype=F32) / HALF)
    ang = pos.astype(F32)[:, None] * inv[None, :]
    return jnp.cos(ang), jnp.sin(ang)


def _rope_tables(pos):
    cos, sin = _rope_angles(pos)
    return (jnp.concatenate([cos, cos, cos, cos], axis=1),
            jnp.concatenate([-sin, sin, -sin, sin], axis=1))


def _row(a):
    return a.reshape(1, -1)


TM = 512
TQ = 256


def kernel(x_prompt, x_sample, cache_k, cache_v, cache_conv, w_in, lambda_q1, lambda_k1, lambda_q2,
           lambda_k2, subln_g, conv_w, conv_b, conv_ln_g, conv_ln_b, w_out, ln1_g, ln1_b, w_ff1,
           w_ff2, ln2_g, ln2_b):
    B, T, _ = x_prompt.shape
    Bs, Ts, _ = x_sample.shape
    P = cache_k.shape[2]
    depth = w_in.shape[0]

    w_in_b = w_in.astype(BF16)
    wk_t = jnp.swapaxes(w_in_b[:, :, D_ATTN:2 * D_ATTN], 1, 2)
    w_out_b = w_out.astype(BF16)
    w_ff1_b = w_ff1.astype(BF16)
    w_ff2_b = w_ff2.astype(BF16)

    pos_p = jnp.arange(T)
    cos_p, sin_p = _rope_tables(pos_p)
    cos_pt, sin_pt = (a.T for a in _rope_angles(pos_p))
    cos_s, sin_s = _rope_tables(P + jnp.arange(Ts))
    reps = TM // Ts
    cos_s = jnp.tile(cos_s, (reps, 1))
    sin_s = jnp.tile(sin_s, (reps, 1))

    xp = x_prompt.reshape(B * T, D_MODEL)
    xs = x_sample.reshape(Bs * Ts, D_MODEL)
    cache_kt = jnp.transpose(cache_k, (0, 1, 3, 4, 2)).reshape(depth, Bs, D_ATTN, P)
    cache_vr = cache_v.reshape(depth, Bs, P * N_DIFF_HEADS, V_DIM)
    hist_s = jnp.pad(cache_conv, ((0, 0), (0, 0), (HALO - CONV_HIST, 0), (0, 0)))

    outs = {k: [] for k in ("cp", "ks", "vs", "cs")}
    kv_prompt = None
    for l in range(depth):
        lam_init = _lambda_init(l)
        lam_params = [_row(lambda_q1[l]), _row(lambda_k1[l]), _row(lambda_q2[l]), _row(lambda_k2[l])]
        g = _row(subln_g[l])
        cw, cb = conv_w[l], _row(conv_b[l])
        cg, cbeta = _row(conv_ln_g[l]), _row(conv_ln_b[l])
        g1, b1 = _row(ln1_g[l]), _row(ln1_b[l])
        g2, b2 = _row(ln2_g[l]), _row(ln2_b[l])

        q, ktf, ktb, vf, vb, u = _in_proj_cols(xp, w_in_b[l], wk_t[l], cos_p, sin_p, cos_pt, sin_pt,
                                               l, depth, B, T, TM, kv_prompt)
        kv_prompt = (ktf, vf)
        attn = _attn_prompt(q, ktb, vb, lam_params, g, lam_init, B, T, TQ)
        xp = _mix(attn, u, u, xp, w_out_b[l], cw, cb, cg, cbeta, g1, b1, TM, T // TM)
        xp = _ffn(xp, w_ff1_b[l], w_ff2_b[l], g2, b2, TM)
        outs["cp"].append(u.reshape(B, T, D_CONV)[:, T - CONV_HIST:])

        q, kf, kb, vf_s, vb, u = _in_proj_rows(xs, w_in_b[l], cos_s, sin_s, TM)
        attn = _attn_sample(q, kb, vb, cache_kt, cache_vr, lam_params, g, lam_init, l, Bs, Ts, P)
        xs = _mix(attn, u, hist_s[l].reshape(Bs * HALO, D_CONV), xs, w_out_b[l], cw, cb, cg, cbeta,
                  g1, b1, Ts, None)
        xs = _ffn(xs, w_ff1_b[l], w_ff2_b[l], g2, b2, TM)
        outs["ks"].append(kf.reshape(Bs, Ts, N_SUB_HEADS, HEAD_DIM))
        outs["vs"].append(vf_s.reshape(Bs, Ts, N_DIFF_HEADS, V_DIM))
        outs["cs"].append(u.reshape(Bs, Ts, D_CONV)[:, Ts - CONV_HIST:])

    ktf, vf = kv_prompt
    new_k_prompt = jnp.transpose(ktf.reshape(depth, B, N_SUB_HEADS, HEAD_DIM, T), (0, 1, 4, 2, 3))
    new_v_prompt = vf.reshape(depth, B, T, N_DIFF_HEADS, V_DIM)
    return (xp.reshape(B, T, D_MODEL), xs.reshape(Bs, Ts, D_MODEL),
            new_k_prompt, new_v_prompt, jnp.stack(outs["cp"]),
            jnp.stack(outs["ks"]), jnp.stack(outs["vs"]), jnp.stack(outs["cs"]))
```

```python
import functools
import math

import jax
import jax.numpy as jnp
from jax import lax
from jax.experimental import pallas as pl
from jax.experimental.pallas import tpu as pltpu

D_MODEL = 1024
D_ATTN = 512
D_CONV = 512
N_DIFF_HEADS = 4
N_SUB_HEADS = 8
HEAD_DIM = 64
HALF = HEAD_DIM // 2
V_DIM = 128
LANES = 128
SUBLANES = 8
CHUNK = 64
CONV_WIDTH = 31
CONV_HIST = CONV_WIDTH - 1
HALO = 32
D_FF = 4096
FF_CHUNK = 1024
ROPE_THETA = 10000.0
LN_EPS = 1e-5
RMS_EPS = 1e-5
DEPTH = 2
DEEPNORM_ALPHA = (2.0 * DEPTH) ** 0.25
NEG_INF = -1e30
LOG2E = math.log2(math.e)
Q_SCALE = HEAD_DIM ** -0.5 * LOG2E
VMEM_LIMIT_BYTES = 56 * 1024 * 1024

BF16 = jnp.bfloat16
F32 = jnp.float32


def _lambda_init(layer):
    return 0.8 - 0.6 * math.exp(-0.3 * layer)


def _params(n_axes):
    return pltpu.CompilerParams(dimension_semantics=("arbitrary",) * n_axes,
                                vmem_limit_bytes=VMEM_LIMIT_BYTES)


def _const_spec(shape):
    return pl.BlockSpec(shape, lambda *_: (0,) * len(shape), pipeline_mode=pl.Buffered(1))


def _layer_norm(z, g, b):
    mu = jnp.mean(z, axis=-1, keepdims=True)
    d = z - mu
    var = jnp.mean(d * d, axis=-1, keepdims=True)
    return d * lax.rsqrt(var + LN_EPS) * g + b


def _dot_nt(a, b):
    return lax.dot_general(a, b, (((1,), (1,)), ((), ())), preferred_element_type=F32)


def _rope_rows(p, cos_ref, sin_ref):
    n = p.shape[1] // LANES
    cos = jnp.concatenate([cos_ref[...]] * n, axis=1)
    sin = jnp.concatenate([sin_ref[...]] * n, axis=1)
    lane = lax.broadcasted_iota(jnp.int32, p.shape, 1)
    partner = jnp.where((lane % HEAD_DIM) < HALF,
                        pltpu.roll(p, p.shape[1] - HALF, 1),
                        pltpu.roll(p, HALF, 1))
    return p * cos + partner * sin


def _rope_cols(pt, cos_ref, sin_ref):
    cos = cos_ref[...]
    sin = sin_ref[...]
    parts = []
    for s in range(pt.shape[0] // HEAD_DIM):
        a = pt[s * HEAD_DIM:s * HEAD_DIM + HALF]
        b = pt[s * HEAD_DIM + HALF:(s + 1) * HEAD_DIM]
        parts += [a * cos - b * sin, b * cos + a * sin]
    return jnp.concatenate(parts, axis=0)


def _in_proj_common(x_ref, w_ref, cos_ref, sin_ref, q_ref, vb_ref, u_ref):
    xb = x_ref[...].astype(BF16)

    def proj(c0, width):
        return jnp.dot(xb, w_ref[:, c0:c0 + width], preferred_element_type=F32)

    q_ref[...] = (_rope_rows(proj(0, D_ATTN), cos_ref, sin_ref) * Q_SCALE).astype(BF16)
    v = proj(2 * D_ATTN, D_ATTN)
    vb_ref[...] = v.astype(BF16)
    ca = proj(3 * D_ATTN, D_CONV)
    cg = proj(3 * D_ATTN + D_CONV, D_CONV)
    u_ref[...] = ca * jax.nn.sigmoid(cg)
    return xb, v, proj


def _in_proj_rows_kernel(x_ref, w_ref, cos_ref, sin_ref, q_ref, kf_ref, kb_ref, vf_ref, vb_ref, u_ref):
    _, v, proj = _in_proj_common(x_ref, w_ref, cos_ref, sin_ref, q_ref, vb_ref, u_ref)
    vf_ref[...] = v
    k = _rope_rows(proj(D_ATTN, D_ATTN), cos_ref, sin_ref)
    kf_ref[...] = k
    kb_ref[...] = k.astype(BF16)


def _in_proj_cols_kernel(x_ref, w_ref, wkt_ref, cos_ref, sin_ref, cost_ref, sint_ref, *rest):
    q_ref, ktf_ref, ktb_ref, vf_ref, vb_ref, u_ref = rest[-6:]
    tm = x_ref.shape[0]
    xb, v, _ = _in_proj_common(x_ref, w_ref, cos_ref, sin_ref, q_ref, vb_ref, u_ref)
    kt = _rope_cols(_dot_nt(wkt_ref[...], xb), cost_ref, sint_ref)
    ktf_ref[...] = kt
    ktb_ref[...] = kt.astype(BF16)
    for h in range(N_DIFF_HEADS):
        vf_ref[pl.ds(h, tm, stride=N_DIFF_HEADS), :] = v[:, h * V_DIM:(h + 1) * V_DIM]


def _in_proj_rows(x, w_in, cos, sin, tm):
    n = x.shape[0]
    n_tab = cos.shape[0] // tm
    tok = lambda width: pl.BlockSpec((tm, width), lambda i: (i, 0))
    tab = pl.BlockSpec((tm, LANES), lambda i: (i % n_tab, 0))
    return pl.pallas_call(
        _in_proj_rows_kernel,
        grid=(n // tm,),
        in_specs=[tok(D_MODEL), _const_spec(w_in.shape), tab, tab],
        out_specs=[tok(D_ATTN)] * 5 + [tok(D_CONV)],
        out_shape=[jax.ShapeDtypeStruct((n, D_ATTN), BF16),
                   jax.ShapeDtypeStruct((n, D_ATTN), F32),
                   jax.ShapeDtypeStruct((n, D_ATTN), BF16),
                   jax.ShapeDtypeStruct((n, D_ATTN), F32),
                   jax.ShapeDtypeStruct((n, D_ATTN), BF16),
                   jax.ShapeDtypeStruct((n, D_CONV), F32)],
        compiler_params=_params(1),
        name="in_proj_rows",
    )(x, w_in, cos, sin)


def _in_proj_cols(x, w_in, wk_t, cos, sin, cos_t, sin_t, layer, depth, batch, seq, tm, prev):
    n = x.shape[0]
    nt = seq // tm
    tok = lambda width: pl.BlockSpec((tm, width), lambda i: (i, 0))
    tab = pl.BlockSpec((tm, LANES), lambda i: (i % nt, 0))
    tab_t = pl.BlockSpec((HALF, tm), lambda i: (0, i % nt))
    in_specs = [tok(D_MODEL), _const_spec(w_in.shape), _const_spec(wk_t.shape), tab, tab, tab_t, tab_t]
    args = [x, w_in, wk_t, cos, sin, cos_t, sin_t]
    aliases = {}
    if prev is not None:
        in_specs += [pl.BlockSpec(memory_space=pl.ANY)] * 2
        aliases = {len(args): 1, len(args) + 1: 3}
        args += list(prev)
    out_specs = [tok(D_ATTN),
                 pl.BlockSpec((None, None, D_ATTN, tm), lambda i: (layer, i // nt, 0, i % nt)),
                 pl.BlockSpec((None, D_ATTN, tm), lambda i: (i // nt, 0, i % nt)),
                 pl.BlockSpec((None, tm * N_DIFF_HEADS, V_DIM), lambda i: (layer, i, 0)),
                 tok(D_ATTN), tok(D_CONV)]
    out_shape = [jax.ShapeDtypeStruct((n, D_ATTN), BF16),
                 jax.ShapeDtypeStruct((depth, batch, D_ATTN, seq), F32),
                 jax.ShapeDtypeStruct((batch, D_ATTN, seq), BF16),
                 jax.ShapeDtypeStruct((depth, n * N_DIFF_HEADS, V_DIM), F32),
                 jax.ShapeDtypeStruct((n, D_ATTN), BF16),
                 jax.ShapeDtypeStruct((n, D_CONV), F32)]
    return pl.pallas_call(
        _in_proj_cols_kernel,
        grid=(n // tm,),
        in_specs=in_specs,
        out_specs=out_specs,
        out_shape=out_shape,
        input_output_aliases=aliases,
        compiler_params=_params(1),
        name="in_proj_cols",
    )(*args)


def _stack_sub_heads(q):
    lane = lax.broadcasted_iota(jnp.int32, q.shape, 1)
    zero = jnp.zeros_like(q)
    return jnp.concatenate([jnp.where(lane < HEAD_DIM, q, zero),
                            jnp.where(lane >= HEAD_DIM, q, zero)], axis=0)


def _lambda(lq1, lk1, lq2, lk2, lam_init):
    s1 = jnp.sum(lq1[...] * lk1[...], axis=1, keepdims=True)
    s2 = jnp.sum(lq2[...] * lk2[...], axis=1, keepdims=True)
    return jnp.exp(s1) - jnp.exp(s2) + lam_init


def _diff_finish(o_all, lam, g, lam_init):
    t = o_all.shape[0] // 2
    o = o_all[:t] - lam * o_all[t:]
    o = o * lax.rsqrt(jnp.mean(o * o, axis=-1, keepdims=True) + RMS_EPS)
    return o * g * (1.0 - lam_init)


def _row_max(s):
    m = s[:, 0:LANES]
    for c in range(1, s.shape[1] // LANES):
        m = jnp.maximum(m, s[:, c * LANES:(c + 1) * LANES])
    return jnp.broadcast_to(jnp.max(m, axis=1, keepdims=True), m.shape)


def _with_ones(v):
    return jnp.concatenate([v, jnp.ones(v.shape, v.dtype)], axis=1)


def _wide(a, width):
    return jnp.concatenate([a] * (width // LANES), axis=1)


def _attn_prompt_kernel(lq1, lk1, lq2, lk2, g_ref, q_ref, kt_ref, v_ref, o_ref, *scratch,
                        lam_init, tq):
    m_scs = scratch[:N_DIFF_HEADS]
    acc_scs = scratch[N_DIFF_HEADS:]
    qi = pl.program_id(1)
    heads = [slice(h * V_DIM, (h + 1) * V_DIM) for h in range(N_DIFF_HEADS)]
    qqs = [_stack_sub_heads(q_ref[:, cols]) for cols in heads]
    for m_sc, acc_sc in zip(m_scs, acc_scs):
        m_sc[...] = jnp.full(m_sc.shape, NEG_INF, F32)
        acc_sc[...] = jnp.zeros(acc_sc.shape, F32)

    def step(kb, masked):
        start = pl.multiple_of(kb * tq, tq)
        if masked:
            shape = (2 * tq, tq)
            q_chunk = (lax.broadcasted_iota(jnp.int32, shape, 0) % tq) // CHUNK
            k_chunk = lax.broadcasted_iota(jnp.int32, shape, 1) // CHUNK
            visible = k_chunk <= q_chunk
        for cols, qq, m_sc, acc_sc in zip(heads, qqs, m_scs, acc_scs):
            kt = kt_ref[cols, pl.ds(start, tq)]
            v = v_ref[pl.ds(start, tq), cols]
            s = jnp.dot(qq, kt, preferred_element_type=F32)
            if masked:
                s = jnp.where(visible, s, NEG_INF)
            m_prev = m_sc[...]
            m_new = jnp.maximum(m_prev, _row_max(s))
            alpha = jnp.exp2(m_prev - m_new)
            p = jnp.exp2(s - _wide(m_new, tq)).astype(BF16)
            pv = jnp.dot(p, _with_ones(v), preferred_element_type=F32)
            acc_sc[...] = _wide(alpha, 2 * V_DIM) * acc_sc[...] + pv
            m_sc[...] = m_new

    def body(kb, carry):
        step(kb, False)
        return carry

    lax.fori_loop(0, qi, body, 0)
    step(qi, True)

    lam = _lambda(lq1, lk1, lq2, lk2, lam_init)
    for cols, acc_sc in zip(heads, acc_scs):
        acc = acc_sc[...]
        o_all = acc[:, :V_DIM] / acc[:, V_DIM:]
        o_ref[:, cols] = _diff_finish(o_all, lam, g_ref[...], lam_init).astype(o_ref.dtype)


def _attn_prompt(q, kt, v, lam_params, g, lam_init, batch, seq, tq):
    nq = seq // tq
    small = pl.BlockSpec((1, HEAD_DIM), lambda b, i: (0, 0))
    kt_spec = pl.BlockSpec((None, D_ATTN, seq), lambda b, i: (b, 0, 0))
    v_spec = pl.BlockSpec((seq, D_ATTN), lambda b, i: (b, 0))
    q_spec = pl.BlockSpec((tq, D_ATTN), lambda b, i: (b * nq + i, 0))
    return pl.pallas_call(
        functools.partial(_attn_prompt_kernel, lam_init=lam_init, tq=tq),
        grid=(batch, nq),
        in_specs=[small] * 4 + [pl.BlockSpec((1, V_DIM), lambda b, i: (0, 0)),
                                q_spec, kt_spec, v_spec],
        out_specs=q_spec,
        out_shape=jax.ShapeDtypeStruct((batch * seq, D_ATTN), BF16),
        scratch_shapes=([pltpu.VMEM((2 * tq, LANES), F32)] * N_DIFF_HEADS
                        + [pltpu.VMEM((2 * tq, 2 * V_DIM), F32)] * N_DIFF_HEADS),
        compiler_params=_params(2),
        name="attn_prompt",
    )(*lam_params, g, q, kt, v)


def _attn_sample_kernel(lq1, lk1, lq2, lk2, g_ref, q_ref, kn_ref, vn_ref, ckt_ref, cv_ref, o_ref,
                        *, lam_init, past):
    lam = _lambda(lq1, lk1, lq2, lk2, lam_init)
    for h in range(N_DIFF_HEADS):
        cols = slice(h * V_DIM, (h + 1) * V_DIM)
        qq = _stack_sub_heads(q_ref[:, cols])
        s_past = jnp.dot(qq, ckt_ref[cols, :].astype(BF16), preferred_element_type=F32)
        s_new = _dot_nt(qq, kn_ref[:, cols])
        m = jnp.maximum(_row_max(s_past), jnp.max(s_new, axis=1, keepdims=True))
        p_past = jnp.exp2(s_past - _wide(m, past)).astype(BF16)
        p_new = jnp.exp2(s_new - m[:, :s_new.shape[1]]).astype(BF16)
        v_past = cv_ref[pl.ds(h, past, stride=N_DIFF_HEADS), :].astype(BF16)
        acc = (jnp.dot(p_past, _with_ones(v_past), preferred_element_type=F32)
               + jnp.dot(p_new, _with_ones(vn_ref[:, cols]), preferred_element_type=F32))
        o_all = acc[:, :V_DIM] / acc[:, V_DIM:]
        o_ref[:, cols] = _diff_finish(o_all, lam, g_ref[...], lam_init).astype(o_ref.dtype)


def _attn_sample(q, k, v, cache_kt, cache_v, lam_params, g, lam_init, layer, batch, seq, past):
    small = pl.BlockSpec((1, HEAD_DIM), lambda b: (0, 0))
    new_spec = pl.BlockSpec((seq, D_ATTN), lambda b: (b, 0))
    return pl.pallas_call(
        functools.partial(_attn_sample_kernel, lam_init=lam_init, past=past),
        grid=(batch,),
        in_specs=[small] * 4 + [pl.BlockSpec((1, V_DIM), lambda b: (0, 0)),
                                new_spec, new_spec, new_spec,
                                pl.BlockSpec((None, None, D_ATTN, past), lambda b: (layer, b, 0, 0)),
                                pl.BlockSpec((None, None, past * N_DIFF_HEADS, V_DIM),
                                             lambda b: (layer, b, 0, 0))],
        out_specs=new_spec,
        out_shape=jax.ShapeDtypeStruct((batch * seq, D_ATTN), BF16),
        compiler_params=_params(1),
        name="attn_sample",
    )(*lam_params, g, q, k, v, cache_kt, cache_v)


CONV_ROWS = 64


def _mix_kernel(attn_ref, u_ref, halo_ref, x_ref, wo_ref, cw_ref, cb_ref, cg_ref, cbeta_ref,
                g1_ref, b1_ref, o_ref, win_sc, c_sc, *, tiles_per_seq):
    tm = u_ref.shape[0]
    halo = halo_ref[...]
    if tiles_per_seq is not None:
        keep = (pl.program_id(0) % tiles_per_seq) != 0
        halo = jnp.where(keep, halo, jnp.zeros_like(halo))
    win_sc[0:HALO, :] = halo
    win_sc[HALO:HALO + tm, :] = u_ref[...]

    off = HALO - CONV_HIST

    def conv_rows(rc, carry):
        base = pl.multiple_of(rc * CONV_ROWS, CONV_ROWS)
        tiles = []
        for lt in range(D_CONV // LANES):
            lanes = slice(lt * LANES, (lt + 1) * LANES)
            x = win_sc[pl.ds(base, CONV_ROWS + HALO), lanes]
            acc = None
            for r in range(SUBLANES):
                rows = CONV_ROWS + SUBLANES if r else CONV_ROWS
                part = None
                for a in range((off + CONV_WIDTH - 1 - r) // SUBLANES + 1):
                    j = SUBLANES * a + r - off
                    if j < 0:
                        continue
                    term = cw_ref[j:j + 1, lanes] * x[SUBLANES * a:SUBLANES * a + rows]
                    part = term if part is None else part + term
                if r:
                    part = pltpu.roll(part, rows - r, 0)[:CONV_ROWS]
                acc = part if acc is None else acc + part
            tiles.append(acc)
        c = jnp.concatenate(tiles, axis=1)
        c = _layer_norm(c + cb_ref[...], cg_ref[...], cbeta_ref[...])
        c = c * jax.nn.sigmoid(c)
        c_sc[pl.ds(base, CONV_ROWS), :] = c.astype(BF16)
        return carry

    lax.fori_loop(0, tm // CONV_ROWS, conv_rows, 0)

    y = (jnp.dot(attn_ref[...], wo_ref[0:D_ATTN, :], preferred_element_type=F32)
         + jnp.dot(c_sc[...], wo_ref[D_ATTN:D_ATTN + D_CONV, :], preferred_element_type=F32))
    o_ref[...] = _layer_norm(DEEPNORM_ALPHA * x_ref[...] + y, g1_ref[...], b1_ref[...])


def _mix(attn, u, halo_src, x, w_out, conv_w, conv_b, conv_g, conv_beta, g1, b1, tm, tiles_per_seq):
    n = x.shape[0]
    tok = lambda width: pl.BlockSpec((tm, width), lambda i: (i, 0))
    if tiles_per_seq is None:
        halo_spec = pl.BlockSpec((HALO, D_CONV), lambda i: (i, 0))
    else:
        halo_spec = pl.BlockSpec((HALO, D_CONV), lambda i: (jnp.maximum(i * (tm // HALO) - 1, 0), 0))
    vec = lambda width: _const_spec((1, width))
    return pl.pallas_call(
        functools.partial(_mix_kernel, tiles_per_seq=tiles_per_seq),
        grid=(n // tm,),
        in_specs=[tok(D_ATTN), tok(D_CONV), halo_spec, tok(D_MODEL), _const_spec(w_out.shape),
                  _const_spec(conv_w.shape), vec(D_CONV), vec(D_CONV), vec(D_CONV),
                  vec(D_MODEL), vec(D_MODEL)],
        out_specs=tok(D_MODEL),
        out_shape=jax.ShapeDtypeStruct((n, D_MODEL), F32),
        scratch_shapes=[pltpu.VMEM((HALO + tm, D_CONV), F32), pltpu.VMEM((tm, D_CONV), BF16)],
        compiler_params=_params(1),
        name="mix",
    )(attn, u, halo_src, x, w_out, conv_w, conv_b, conv_g, conv_beta, g1, b1)


def _ffn_kernel(x_ref, w1_ref, w2_ref, g_ref, b_ref, o_ref):
    x = x_ref[...]
    xb = x.astype(BF16)
    acc = jnp.zeros(x.shape, F32)
    for c in range(D_FF // FF_CHUNK):
        h = jnp.dot(xb, w1_ref[:, c * FF_CHUNK:(c + 1) * FF_CHUNK], preferred_element_type=F32)
        h = jnp.maximum(h, 0.0)
        h = (h * h).astype(BF16)
        acc = acc + jnp.dot(h, w2_ref[c * FF_CHUNK:(c + 1) * FF_CHUNK, :], preferred_element_type=F32)
    o_ref[...] = _layer_norm(DEEPNORM_ALPHA * x + acc, g_ref[...], b_ref[...])


def _ffn(x, w1, w2, g, b, tm):
    n = x.shape[0]
    tok = pl.BlockSpec((tm, D_MODEL), lambda i: (i, 0))
    return pl.pallas_call(
        _ffn_kernel,
        grid=(n // tm,),
        in_specs=[tok, _const_spec(w1.shape), _const_spec(w2.shape),
                  _const_spec((1, D_MODEL)), _const_spec((1, D_MODEL))],
        out_specs=tok,
        out_shape=jax.ShapeDtypeStruct((n, D_MODEL), F32),
        compiler_params=_params(1),
        name="ffn",
    )(x, w1, w2, g, b)


def _rope_angles(pos):
    inv = ROPE_THETA ** (-jnp.arange(HALF, dtype=F32) / HALF)
    ang = pos.astype(F32)[:, None] * inv[None, :]
    return jnp.cos(ang), jnp.sin(ang)


def _rope_tables(pos):
    cos, sin = _rope_angles(pos)
    return (jnp.concatenate([cos, cos, cos, cos], axis=1),
            jnp.concatenate([-sin, sin, -sin, sin], axis=1))


def _row(a):
    return a.reshape(1, -1)


TM = 512
TQ = 256


def kernel(x_prompt, x_sample, cache_k, cache_v, cache_conv, w_in, lambda_q1, lambda_k1, lambda_q2,
           lambda_k2, subln_g, conv_w, conv_b, conv_ln_g, conv_ln_b, w_out, ln1_g, ln1_b, w_ff1,
           w_ff2, ln2_g, ln2_b):
    B, T, _ = x_prompt.shape
    Bs, Ts, _ = x_sample.shape
    P = cache_k.shape[2]
    depth = w_in.shape[0]

    w_in_b = w_in.astype(BF16)
    wk_t = jnp.swapaxes(w_in_b[:, :, D_ATTN:2 * D_ATTN], 1, 2)
    w_out_b = w_out.astype(BF16)
    w_ff1_b = w_ff1.astype(BF16)
    w_ff2_b = w_ff2.astype(BF16)

    pos_p = jnp.arange(T)
    cos_p, sin_p = _rope_tables(pos_p)
    cos_pt, sin_pt = (a.T for a in _rope_angles(pos_p))
    cos_s, sin_s = _rope_tables(P + jnp.arange(Ts))
    reps = TM // Ts
    cos_s = jnp.tile(cos_s, (reps, 1))
    sin_s = jnp.tile(sin_s, (reps, 1))

    xp = x_prompt.reshape(B * T, D_MODEL)
    xs = x_sample.reshape(Bs * Ts, D_MODEL)
    cache_kt = jnp.transpose(cache_k, (0, 1, 3, 4, 2)).reshape(depth, Bs, D_ATTN, P)
    cache_vr = cache_v.reshape(depth, Bs, P * N_DIFF_HEADS, V_DIM)
    hist_s = jnp.pad(cache_conv, ((0, 0), (0, 0), (HALO - CONV_HIST, 0), (0, 0)))

    outs = {k: [] for k in ("cp", "ks", "vs", "cs")}
    kv_prompt = None
    for l in range(depth):
        lam_init = _lambda_init(l)
        lam_params = [_row(lambda_q1[l]), _row(lambda_k1[l]), _row(lambda_q2[l]), _row(lambda_k2[l])]
        g = _row(subln_g[l])
        cw, cb = conv_w[l], _row(conv_b[l])
        cg, cbeta = _row(conv_ln_g[l]), _row(conv_ln_b[l])
        g1, b1 = _row(ln1_g[l]), _row(ln1_b[l])
        g2, b2 = _row(ln2_g[l]), _row(ln2_b[l])

        q, ktf, ktb, vf, vb, u = _in_proj_cols(xp, w_in_b[l], wk_t[l], cos_p, sin_p, cos_pt, sin_pt,
                                               l, depth, B, T, TM, kv_prompt)
        kv_prompt = (ktf, vf)
        attn = _attn_prompt(q, ktb, vb, lam_params, g, lam_init, B, T, TQ)
        xp = _mix(attn, u, u, xp, w_out_b[l], cw, cb, cg, cbeta, g1, b1, TM, T // TM)
        xp = _ffn(xp, w_ff1_b[l], w_ff2_b[l], g2, b2, TM)
        outs["cp"].append(u.reshape(B, T, D_CONV)[:, T - CONV_HIST:])

        q, kf, kb, vf_s, vb, u = _in_proj_rows(xs, w_in_b[l], cos_s, sin_s, TM)
        attn = _attn_sample(q, kb, vb, cache_kt, cache_vr, lam_params, g, lam_init, l, Bs, Ts, P)
        xs = _mix(attn, u, hist_s[l].reshape(Bs * HALO, D_CONV), xs, w_out_b[l], cw, cb, cg, cbeta,
                  g1, b1, Ts, None)
        xs = _ffn(xs, w_ff1_b[l], w_ff2_b[l], g2, b2, TM)
        outs["ks"].append(kf.reshape(Bs, Ts, N_SUB_HEADS, HEAD_DIM))
        outs["vs"].append(vf_s.reshape(Bs, Ts, N_DIFF_HEADS, V_DIM))
        outs["cs"].append(u.reshape(Bs, Ts, D_CONV)[:, Ts - CONV_HIST:])

    ktf, vf = kv_prompt
    new_k_prompt = jnp.transpose(ktf.reshape(depth, B, N_SUB_HEADS, HEAD_DIM, T), (0, 1, 4, 2, 3))
    new_v_prompt = vf.reshape(depth, B, T, N_DIFF_HEADS, V_DIM)
    return (xp.reshape(B, T, D_MODEL), xs.reshape(Bs, Ts, D_MODEL),
            new_k_prompt, new_v_prompt, jnp.stack(outs["cp"]),
            jnp.stack(outs["ks"]), jnp.stack(outs["vs"]), jnp.stack(outs["cs"]))
```

```python
import functools
import math

import jax
import jax.numpy as jnp
from jax import lax
from jax.experimental import pallas as pl
from jax.experimental.pallas import tpu as pltpu

D_MODEL = 1024
D_ATTN = 512
D_CONV = 512
N_DIFF_HEADS = 4
N_SUB_HEADS = 8
HEAD_DIM = 64
HALF = HEAD_DIM // 2
V_DIM = 128
LANES = 128
SUBLANES = 8
CHUNK = 64
CONV_WIDTH = 31
CONV_HIST = CONV_WIDTH - 1
HALO = 32
CONV_ROWS = 64
D_FF = 4096
FF_CHUNK = 1024
ROPE_THETA = 10000.0
LN_EPS = 1e-5
RMS_EPS = 1e-5
DEPTH = 2
DEEPNORM_ALPHA = (2.0 * DEPTH) ** 0.25
NEG_INF = -1e30
LOG2E = math.log2(math.e)
Q_SCALE = HEAD_DIM ** -0.5 * LOG2E
VMEM_LIMIT_BYTES = 56 * 1024 * 1024

BF16 = jnp.bfloat16
F32 = jnp.float32


def _lambda_init(layer):
    return 0.8 - 0.6 * math.exp(-0.3 * layer)


def _params(n_axes):
    return pltpu.CompilerParams(dimension_semantics=("arbitrary",) * n_axes,
                                vmem_limit_bytes=VMEM_LIMIT_BYTES)


def _const_spec(shape):
    return pl.BlockSpec(shape, lambda *_: (0,) * len(shape), pipeline_mode=pl.Buffered(1))


def _layer_norm(z, g, b):
    mu = jnp.mean(z, axis=-1, keepdims=True)
    d = z - mu
    var = jnp.mean(d * d, axis=-1, keepdims=True)
    return d * lax.rsqrt(var + LN_EPS) * g + b


def _dot_nt(a, b):
    return lax.dot_general(a, b, (((1,), (1,)), ((), ())), preferred_element_type=F32)


def _conv_chunk(win_ref, base, cw_ref, cb_ref, cg_ref, cbeta_ref):
    off = HALO - CONV_HIST
    tiles = []
    for lt in range(D_CONV // LANES):
        lanes = slice(lt * LANES, (lt + 1) * LANES)
        x = win_ref[pl.ds(base, CONV_ROWS + HALO), lanes]
        acc = None
        for r in range(SUBLANES):
            rows = CONV_ROWS + SUBLANES if r else CONV_ROWS
            part = None
            for a in range((off + CONV_WIDTH - 1 - r) // SUBLANES + 1):
                j = SUBLANES * a + r - off
                if j < 0:
                    continue
                term = cw_ref[j:j + 1, lanes] * x[SUBLANES * a:SUBLANES * a + rows]
                part = term if part is None else part + term
            if r:
                part = pltpu.roll(part, rows - r, 0)[:CONV_ROWS]
            acc = part if acc is None else acc + part
        tiles.append(acc)
    c = jnp.concatenate(tiles, axis=1)
    c = _layer_norm(c + cb_ref[...], cg_ref[...], cbeta_ref[...])
    return c * jax.nn.sigmoid(c)


def _conv_rows_kernel(u_ref, halo_ref, cw_ref, cb_ref, cg_ref, cbeta_ref, c_ref, win_sc):
    tm = u_ref.shape[0]
    win_sc[0:HALO, :] = halo_ref[...]
    win_sc[HALO:HALO + tm, :] = u_ref[...]
    for rc in range(tm // CONV_ROWS):
        base = rc * CONV_ROWS
        c_ref[base:base + CONV_ROWS, :] = _conv_chunk(win_sc, base, cw_ref, cb_ref, cg_ref,
                                                      cbeta_ref).astype(BF16)


def _conv_rows(u, halo, conv_w, conv_b, conv_g, conv_beta, tm):
    n = u.shape[0]
    vec = _const_spec((1, D_CONV))
    return pl.pallas_call(
        _conv_rows_kernel,
        grid=(n // tm,),
        in_specs=[pl.BlockSpec((tm, D_CONV), lambda i: (i, 0)),
                  pl.BlockSpec((HALO, D_CONV), lambda i: (i, 0)),
                  _const_spec(conv_w.shape), vec, vec, vec],
        out_specs=pl.BlockSpec((tm, D_CONV), lambda i: (i, 0)),
        out_shape=jax.ShapeDtypeStruct((n, D_CONV), BF16),
        scratch_shapes=[pltpu.VMEM((HALO + tm, D_CONV), F32)],
        compiler_params=_params(1),
        name="conv_rows",
    )(u, halo, conv_w, conv_b, conv_g, conv_beta)


def _rope_rows(p, cos_ref, sin_ref):
    n = p.shape[1] // LANES
    cos = jnp.concatenate([cos_ref[...]] * n, axis=1)
    sin = jnp.concatenate([sin_ref[...]] * n, axis=1)
    lane = lax.broadcasted_iota(jnp.int32, p.shape, 1)
    partner = jnp.where((lane % HEAD_DIM) < HALF,
                        pltpu.roll(p, p.shape[1] - HALF, 1),
                        pltpu.roll(p, HALF, 1))
    return p * cos + partner * sin


def _rope_cols(pt, cos_ref, sin_ref):
    cos = cos_ref[...]
    sin = sin_ref[...]
    parts = []
    for s in range(pt.shape[0] // HEAD_DIM):
        a = pt[s * HEAD_DIM:s * HEAD_DIM + HALF]
        b = pt[s * HEAD_DIM + HALF:(s + 1) * HEAD_DIM]
        parts += [a * cos - b * sin, b * cos + a * sin]
    return jnp.concatenate(parts, axis=0)


def _proj(xb, w_ref, c0, width):
    return jnp.dot(xb, w_ref[:, c0:c0 + width], preferred_element_type=F32)


def _glu(xb, w_ref):
    ca = _proj(xb, w_ref, 3 * D_ATTN, D_CONV)
    cg = _proj(xb, w_ref, 3 * D_ATTN + D_CONV, D_CONV)
    return ca * jax.nn.sigmoid(cg)


def _in_proj_rows_kernel(x_ref, w_ref, cos_ref, sin_ref, q_ref, kf_ref, kb_ref, vf_ref, vb_ref, u_ref):
    xb = x_ref[...].astype(BF16)
    q_ref[...] = (_rope_rows(_proj(xb, w_ref, 0, D_ATTN), cos_ref, sin_ref) * Q_SCALE).astype(BF16)
    k = _rope_rows(_proj(xb, w_ref, D_ATTN, D_ATTN), cos_ref, sin_ref)
    kf_ref[...] = k
    kb_ref[...] = k.astype(BF16)
    v = _proj(xb, w_ref, 2 * D_ATTN, D_ATTN)
    vf_ref[...] = v
    vb_ref[...] = v.astype(BF16)
    u_ref[...] = _glu(xb, w_ref)


def _in_proj_rows(x, w_in, cos, sin, tm):
    n = x.shape[0]
    n_tab = cos.shape[0] // tm
    tok = lambda width: pl.BlockSpec((tm, width), lambda i: (i, 0))
    tab = pl.BlockSpec((tm, LANES), lambda i: (i % n_tab, 0))
    return pl.pallas_call(
        _in_proj_rows_kernel,
        grid=(n // tm,),
        in_specs=[tok(D_MODEL), _const_spec(w_in.shape), tab, tab],
        out_specs=[tok(D_ATTN)] * 5 + [tok(D_CONV)],
        out_shape=[jax.ShapeDtypeStruct((n, D_ATTN), BF16),
                   jax.ShapeDtypeStruct((n, D_ATTN), F32),
                   jax.ShapeDtypeStruct((n, D_ATTN), BF16),
                   jax.ShapeDtypeStruct((n, D_ATTN), F32),
                   jax.ShapeDtypeStruct((n, D_ATTN), BF16),
                   jax.ShapeDtypeStruct((n, D_CONV), F32)],
        compiler_params=_params(1),
        name="in_proj_rows",
    )(x, w_in, cos, sin)


def _in_proj_conv_kernel(x_ref, w_ref, wkt_ref, cos_ref, sin_ref, cost_ref, sint_ref,
                         cw_ref, cb_ref, cg_ref, cbeta_ref, *rest, tiles_per_seq):
    q_ref, ktf_ref, ktb_ref, vf_ref, vb_ref, c_ref, tail_ref, win_sc = rest[-8:]
    tm = x_ref.shape[0]
    first = (pl.program_id(0) % tiles_per_seq) == 0

    @pl.when(first)
    def _():
        win_sc[0:HALO, :] = jnp.zeros((HALO, D_CONV), F32)

    @pl.when(jnp.logical_not(first))
    def _():
        win_sc[0:HALO, :] = win_sc[tm:tm + HALO, :]

    xb = x_ref[...].astype(BF16)
    u = _glu(xb, w_ref)
    win_sc[HALO:HALO + tm, :] = u
    tail_ref[...] = u[tm - HALO:tm]
    for rc in range(tm // CONV_ROWS):
        base = rc * CONV_ROWS
        c_ref[base:base + CONV_ROWS, :] = _conv_chunk(win_sc, base, cw_ref, cb_ref, cg_ref,
                                                      cbeta_ref).astype(BF16)

    q_ref[...] = (_rope_rows(_proj(xb, w_ref, 0, D_ATTN), cos_ref, sin_ref) * Q_SCALE).astype(BF16)
    kt = _rope_cols(_dot_nt(wkt_ref[...], xb), cost_ref, sint_ref)
    ktf_ref[...] = kt
    ktb_ref[...] = kt.astype(BF16)
    v = _proj(xb, w_ref, 2 * D_ATTN, D_ATTN)
    vb_ref[...] = v.astype(BF16)
    for h in range(N_DIFF_HEADS):
        vf_ref[pl.ds(h, tm, stride=N_DIFF_HEADS), :] = v[:, h * V_DIM:(h + 1) * V_DIM]


def _in_proj_conv(x, w_in, wk_t, cos, sin, cos_t, sin_t, conv_w, conv_b, conv_g, conv_beta,
                  layer, depth, batch, seq, tm, prev):
    n = x.shape[0]
    nt = seq // tm
    tok = lambda width: pl.BlockSpec((tm, width), lambda i: (i, 0))
    tab = pl.BlockSpec((tm, LANES), lambda i: (i % nt, 0))
    tab_t = pl.BlockSpec((HALF, tm), lambda i: (0, i % nt))
    vec = _const_spec((1, D_CONV))
    in_specs = [tok(D_MODEL), _const_spec(w_in.shape), _const_spec(wk_t.shape), tab, tab, tab_t, tab_t,
                _const_spec(conv_w.shape), vec, vec, vec]
    args = [x, w_in, wk_t, cos, sin, cos_t, sin_t, conv_w, conv_b, conv_g, conv_beta]
    aliases = {}
    if prev is not None:
        in_specs += [pl.BlockSpec(memory_space=pl.ANY)] * 2
        aliases = {len(args): 1, len(args) + 1: 3}
        args += list(prev)
    out_specs = [tok(D_ATTN),
                 pl.BlockSpec((None, None, D_ATTN, tm), lambda i: (layer, i // nt, 0, i % nt)),
                 pl.BlockSpec((None, D_ATTN, tm), lambda i: (i // nt, 0, i % nt)),
                 pl.BlockSpec((None, tm * N_DIFF_HEADS, V_DIM), lambda i: (layer, i, 0)),
                 tok(D_ATTN), tok(D_CONV),
                 pl.BlockSpec((None, HALO, D_CONV), lambda i: (i // nt, 0, 0))]
    out_shape = [jax.ShapeDtypeStruct((n, D_ATTN), BF16),
                 jax.ShapeDtypeStruct((depth, batch, D_ATTN, seq), F32),
                 jax.ShapeDtypeStruct((batch, D_ATTN, seq), BF16),
                 jax.ShapeDtypeStruct((depth, n * N_DIFF_HEADS, V_DIM), F32),
                 jax.ShapeDtypeStruct((n, D_ATTN), BF16),
                 jax.ShapeDtypeStruct((n, D_CONV), BF16),
                 jax.ShapeDtypeStruct((batch, HALO, D_CONV), F32)]
    return pl.pallas_call(
        functools.partial(_in_proj_conv_kernel, tiles_per_seq=nt),
        grid=(n // tm,),
        in_specs=in_specs,
        out_specs=out_specs,
        out_shape=out_shape,
        input_output_aliases=aliases,
        scratch_shapes=[pltpu.VMEM((HALO + tm, D_CONV), F32)],
        compiler_params=_params(1),
        name="in_proj_conv",
    )(*args)


def _stack_sub_heads(q):
    lane = lax.broadcasted_iota(jnp.int32, q.shape, 1)
    zero = jnp.zeros_like(q)
    return jnp.concatenate([jnp.where(lane < HEAD_DIM, q, zero),
                            jnp.where(lane >= HEAD_DIM, q, zero)], axis=0)


def _lambda(lq1, lk1, lq2, lk2, lam_init):
    s1 = jnp.sum(lq1[...] * lk1[...], axis=1, keepdims=True)
    s2 = jnp.sum(lq2[...] * lk2[...], axis=1, keepdims=True)
    return jnp.exp(s1) - jnp.exp(s2) + lam_init


def _diff_finish(o_all, lam, g, lam_init):
    t = o_all.shape[0] // 2
    o = o_all[:t] - lam * o_all[t:]
    o = o * lax.rsqrt(jnp.mean(o * o, axis=-1, keepdims=True) + RMS_EPS)
    return o * g * (1.0 - lam_init)


def _row_max(s):
    m = s[:, 0:LANES]
    for c in range(1, s.shape[1] // LANES):
        m = jnp.maximum(m, s[:, c * LANES:(c + 1) * LANES])
    return jnp.broadcast_to(jnp.max(m, axis=1, keepdims=True), m.shape)


def _with_ones(v):
    return jnp.concatenate([v, jnp.ones(v.shape, v.dtype)], axis=1)


def _wide(a, width):
    return jnp.concatenate([a] * (width // LANES), axis=1)


def _attn_prompt_kernel(lq1, lk1, lq2, lk2, g_ref, q_ref, kt_ref, v_ref, o_ref, *scratch,
                        lam_init, tq):
    m_scs = scratch[:N_DIFF_HEADS]
    acc_scs = scratch[N_DIFF_HEADS:]
    qi = pl.program_id(1)
    heads = [slice(h * V_DIM, (h + 1) * V_DIM) for h in range(N_DIFF_HEADS)]
    qqs = [_stack_sub_heads(q_ref[:, cols]) for cols in heads]
    for m_sc, acc_sc in zip(m_scs, acc_scs):
        m_sc[...] = jnp.full(m_sc.shape, NEG_INF, F32)
        acc_sc[...] = jnp.zeros(acc_sc.shape, F32)

    def step(start, width, masked):
        if masked:
            shape = (2 * tq, width)
            q_chunk = (lax.broadcasted_iota(jnp.int32, shape, 0) % tq) // CHUNK
            k_chunk = lax.broadcasted_iota(jnp.int32, shape, 1) // CHUNK
            visible = k_chunk <= q_chunk
        for cols, qq, m_sc, acc_sc in zip(heads, qqs, m_scs, acc_scs):
            kt = kt_ref[cols, pl.ds(start, width)]
            v = v_ref[pl.ds(start, width), cols]
            s = jnp.dot(qq, kt, preferred_element_type=F32)
            if masked:
                s = jnp.where(visible, s, NEG_INF)
            m_prev = m_sc[...]
            m_new = jnp.maximum(m_prev, _row_max(s))
            alpha = jnp.exp2(m_prev - m_new)
            p = jnp.exp2(s - _wide(m_new, width)).astype(BF16)
            pv = jnp.dot(p, _with_ones(v), preferred_element_type=F32)
            acc_sc[...] = _wide(alpha, 2 * V_DIM) * acc_sc[...] + pv
            m_sc[...] = m_new

    def body(kb, carry):
        step(pl.multiple_of(kb * 2 * tq, 2 * tq), 2 * tq, False)
        return carry

    lax.fori_loop(0, qi // 2, body, 0)

    @pl.when(qi % 2 == 1)
    def _():
        step(pl.multiple_of((qi - 1) * tq, tq), tq, False)

    step(pl.multiple_of(qi * tq, tq), tq, True)

    lam = _lambda(lq1, lk1, lq2, lk2, lam_init)
    for cols, acc_sc in zip(heads, acc_scs):
        acc = acc_sc[...]
        o_all = acc[:, :V_DIM] / acc[:, V_DIM:]
        o_ref[:, cols] = _diff_finish(o_all, lam, g_ref[...], lam_init).astype(o_ref.dtype)


def _attn_prompt(q, kt, v, lam_params, g, lam_init, batch, seq, tq):
    nq = seq // tq
    small = pl.BlockSpec((1, HEAD_DIM), lambda b, i: (0, 0))
    kt_spec = pl.BlockSpec((None, D_ATTN, seq), lambda b, i: (b, 0, 0))
    v_spec = pl.BlockSpec((seq, D_ATTN), lambda b, i: (b, 0))
    q_spec = pl.BlockSpec((tq, D_ATTN), lambda b, i: (b * nq + i, 0))
    return pl.pallas_call(
        functools.partial(_attn_prompt_kernel, lam_init=lam_init, tq=tq),
        grid=(batch, nq),
        in_specs=[small] * 4 + [pl.BlockSpec((1, V_DIM), lambda b, i: (0, 0)),
                                q_spec, kt_spec, v_spec],
        out_specs=q_spec,
        out_shape=jax.ShapeDtypeStruct((batch * seq, D_ATTN), BF16),
        scratch_shapes=([pltpu.VMEM((2 * tq, LANES), F32)] * N_DIFF_HEADS
                        + [pltpu.VMEM((2 * tq, 2 * V_DIM), F32)] * N_DIFF_HEADS),
        compiler_params=_params(2),
        name="attn_prompt",
    )(*lam_params, g, q, kt, v)


def _attn_sample_kernel(lq1, lk1, lq2, lk2, g_ref, q_ref, kn_ref, vn_ref, ckt_ref, cv_ref, o_ref,
                        *, lam_init, past):
    lam = _lambda(lq1, lk1, lq2, lk2, lam_init)
    for h in range(N_DIFF_HEADS):
        cols = slice(h * V_DIM, (h + 1) * V_DIM)
        qq = _stack_sub_heads(q_ref[:, cols])
        s_past = jnp.dot(qq, ckt_ref[cols, :].astype(BF16), preferred_element_type=F32)
        s_new = _dot_nt(qq, kn_ref[:, cols])
        m = jnp.maximum(_row_max(s_past), jnp.max(s_new, axis=1, keepdims=True))
        p_past = jnp.exp2(s_past - _wide(m, past)).astype(BF16)
        p_new = jnp.exp2(s_new - m[:, :s_new.shape[1]]).astype(BF16)
        v_past = cv_ref[pl.ds(h, past, stride=N_DIFF_HEADS), :].astype(BF16)
        acc = (jnp.dot(p_past, _with_ones(v_past), preferred_element_type=F32)
               + jnp.dot(p_new, _with_ones(vn_ref[:, cols]), preferred_element_type=F32))
        o_all = acc[:, :V_DIM] / acc[:, V_DIM:]
        o_ref[:, cols] = _diff_finish(o_all, lam, g_ref[...], lam_init).astype(o_ref.dtype)


def _attn_sample(q, k, v, cache_kt, cache_v, lam_params, g, lam_init, layer, batch, seq, past):
    small = pl.BlockSpec((1, HEAD_DIM), lambda b: (0, 0))
    new_spec = pl.BlockSpec((seq, D_ATTN), lambda b: (b, 0))
    return pl.pallas_call(
        functools.partial(_attn_sample_kernel, lam_init=lam_init, past=past),
        grid=(batch,),
        in_specs=[small] * 4 + [pl.BlockSpec((1, V_DIM), lambda b: (0, 0)),
                                new_spec, new_spec, new_spec,
                                pl.BlockSpec((None, None, D_ATTN, past), lambda b: (layer, b, 0, 0)),
                                pl.BlockSpec((None, None, past * N_DIFF_HEADS, V_DIM),
                                             lambda b: (layer, b, 0, 0))],
        out_specs=new_spec,
        out_shape=jax.ShapeDtypeStruct((batch * seq, D_ATTN), BF16),
        compiler_params=_params(1),
        name="attn_sample",
    )(*lam_params, g, q, k, v, cache_kt, cache_v)


def _post_kernel(attn_ref, c_ref, x_ref, wo_ref, g1_ref, b1_ref, w1_ref, w2_ref, g2_ref, b2_ref, o_ref):
    y = (jnp.dot(attn_ref[...], wo_ref[0:D_ATTN, :], preferred_element_type=F32)
         + jnp.dot(c_ref[...], wo_ref[D_ATTN:D_ATTN + D_CONV, :], preferred_element_type=F32))
    x1 = _layer_norm(DEEPNORM_ALPHA * x_ref[...] + y, g1_ref[...], b1_ref[...])
    xb = x1.astype(BF16)
    acc = jnp.zeros(x1.shape, F32)
    for c in range(D_FF // FF_CHUNK):
        h = jnp.dot(xb, w1_ref[:, c * FF_CHUNK:(c + 1) * FF_CHUNK], preferred_element_type=F32)
        h = jnp.maximum(h, 0.0)
        h = (h * h).astype(BF16)
        acc = acc + jnp.dot(h, w2_ref[c * FF_CHUNK:(c + 1) * FF_CHUNK, :], preferred_element_type=F32)
    o_ref[...] = _layer_norm(DEEPNORM_ALPHA * x1 + acc, g2_ref[...], b2_ref[...])


def _post(attn, c, x, w_out, g1, b1, w1, w2, g2, b2, tm):
    n = x.shape[0]
    tok = lambda width: pl.BlockSpec((tm, width), lambda i: (i, 0))
    vec = _const_spec((1, D_MODEL))
    return pl.pallas_call(
        _post_kernel,
        grid=(n // tm,),
        in_specs=[tok(D_ATTN), tok(D_CONV), tok(D_MODEL), _const_spec(w_out.shape), vec, vec,
                  _const_spec(w1.shape), _const_spec(w2.shape), vec, vec],
        out_specs=tok(D_MODEL),
        out_shape=jax.ShapeDtypeStruct((n, D_MODEL), F32),
        compiler_params=_params(1),
        name="post",
    )(attn, c, x, w_out, g1, b1, w1, w2, g2, b2)


def _rope_angles(pos):
    inv = ROPE_THETA ** (-jnp.arange(HALF, dtype=F32) / HALF)
    ang = pos.astype(F32)[:, None] * inv[None, :]
    return jnp.cos(ang), jnp.sin(ang)


def _rope_tables(pos):
    cos, sin = _rope_angles(pos)
    return (jnp.concatenate([cos, cos, cos, cos], axis=1),
            jnp.concatenate([-sin, sin, -sin, sin], axis=1))


def _row(a):
    return a.reshape(1, -1)


TM = 512
TQ = 256


def kernel(x_prompt, x_sample, cache_k, cache_v, cache_conv, w_in, lambda_q1, lambda_k1, lambda_q2,
           lambda_k2, subln_g, conv_w, conv_b, conv_ln_g, conv_ln_b, w_out, ln1_g, ln1_b, w_ff1,
           w_ff2, ln2_g, ln2_b):
    B, T, _ = x_prompt.shape
    Bs, Ts, _ = x_sample.shape
    P = cache_k.shape[2]
    depth = w_in.shape[0]

    w_in_b = w_in.astype(BF16)
    wk_t = jnp.swapaxes(w_in_b[:, :, D_ATTN:2 * D_ATTN], 1, 2)
    w_out_b = w_out.astype(BF16)
    w_ff1_b = w_ff1.astype(BF16)
    w_ff2_b = w_ff2.astype(BF16)

    pos_p = jnp.arange(T)
    cos_p, sin_p = _rope_tables(pos_p)
    cos_pt, sin_pt = (a.T for a in _rope_angles(pos_p))
    cos_s, sin_s = _rope_tables(P + jnp.arange(Ts))
    reps = TM // Ts
    cos_s = jnp.tile(cos_s, (reps, 1))
    sin_s = jnp.tile(sin_s, (reps, 1))

    xp = x_prompt.reshape(B * T, D_MODEL)
    xs = x_sample.reshape(Bs * Ts, D_MODEL)
    cache_kt = jnp.transpose(cache_k, (0, 1, 3, 4, 2)).reshape(depth, Bs, D_ATTN, P)
    cache_vr = cache_v.reshape(depth, Bs, P * N_DIFF_HEADS, V_DIM)
    hist_s = jnp.pad(cache_conv, ((0, 0), (0, 0), (HALO - CONV_HIST, 0), (0, 0)))

    outs = {k: [] for k in ("cp", "ks", "vs", "cs")}
    kv_prompt = None
    for l in range(depth):
        lam_init = _lambda_init(l)
        lam_params = [_row(lambda_q1[l]), _row(lambda_k1[l]), _row(lambda_q2[l]), _row(lambda_k2[l])]
        g = _row(subln_g[l])
        conv_params = (conv_w[l], _row(conv_b[l]), _row(conv_ln_g[l]), _row(conv_ln_b[l]))
        post_params = (w_out_b[l], _row(ln1_g[l]), _row(ln1_b[l]), w_ff1_b[l], w_ff2_b[l],
                       _row(ln2_g[l]), _row(ln2_b[l]))

        q, ktf, ktb, vf, vb, c, tail = _in_proj_conv(xp, w_in_b[l], wk_t[l], cos_p, sin_p, cos_pt, sin_pt,
                                                     *conv_params, l, depth, B, T, TM, kv_prompt)
        kv_prompt = (ktf, vf)
        attn = _attn_prompt(q, ktb, vb, lam_params, g, lam_init, B, T, TQ)
        xp = _post(attn, c, xp, *post_params, TM)
        outs["cp"].append(tail[:, HALO - CONV_HIST:])

        q, kf, kb, vf_s, vb, u = _in_proj_rows(xs, w_in_b[l], cos_s, sin_s, TM)
        attn = _attn_sample(q, kb, vb, cache_kt, cache_vr, lam_params, g, lam_init, l, Bs, Ts, P)
        c = _conv_rows(u, hist_s[l].reshape(Bs * HALO, D_CONV), *conv_params, Ts)
        xs = _post(attn, c, xs, *post_params, TM)
        outs["ks"].append(kf.reshape(Bs, Ts, N_SUB_HEADS, HEAD_DIM))
        outs["vs"].append(vf_s.reshape(Bs, Ts, N_DIFF_HEADS, V_DIM))
        outs["cs"].append(u.reshape(Bs, Ts, D_CONV)[:, Ts - CONV_HIST:])

    ktf, vf = kv_prompt
    new_k_prompt = jnp.transpose(ktf.reshape(depth, B, N_SUB_HEADS, HEAD_DIM, T), (0, 1, 4, 2, 3))
    new_v_prompt = vf.reshape(depth, B, T, N_DIFF_HEADS, V_DIM)
    return (xp.reshape(B, T, D_MODEL), xs.reshape(Bs, Ts, D_MODEL),
            new_k_prompt, new_v_prompt, jnp.stack(outs["cp"]),
            jnp.stack(outs["ks"]), jnp.stack(outs["vs"]), jnp.stack(outs["cs"]))
```

```python
import functools
import math

import jax
import jax.numpy as jnp
from jax import lax
from jax.experimental import pallas as pl
from jax.experimental.pallas import tpu as pltpu

D_MODEL = 1024
D_ATTN = 512
D_CONV = 512
N_DIFF_HEADS = 4
N_SUB_HEADS = 8
HEAD_DIM = 64
HALF = HEAD_DIM // 2
V_DIM = 128
LANES = 128
SUBLANES = 8
CHUNK = 64
CONV_WIDTH = 31
CONV_HIST = CONV_WIDTH - 1
HALO = 32
CONV_ROWS = 64
D_FF = 4096
FF_CHUNK = 1024
ROPE_THETA = 10000.0
LN_EPS = 1e-5
RMS_EPS = 1e-5
DEPTH = 2
DEEPNORM_ALPHA = (2.0 * DEPTH) ** 0.25
NEG_INF = -1e30
LOG2E = math.log2(math.e)
Q_SCALE = HEAD_DIM ** -0.5 * LOG2E
VMEM_LIMIT_BYTES = 56 * 1024 * 1024

BF16 = jnp.bfloat16
F32 = jnp.float32


def _lambda_init(layer):
    return 0.8 - 0.6 * math.exp(-0.3 * layer)


def _params(n_axes):
    return pltpu.CompilerParams(dimension_semantics=("arbitrary",) * n_axes,
                                vmem_limit_bytes=VMEM_LIMIT_BYTES)


def _layer_spec(param, layer):
    rest = param.shape[1:]
    return pl.BlockSpec((None,) + rest, lambda *_: (layer,) + (0,) * len(rest),
                        pipeline_mode=pl.Buffered(1))


def _layer_norm(z, g, b):
    mu = jnp.mean(z, axis=-1, keepdims=True)
    d = z - mu
    var = jnp.mean(d * d, axis=-1, keepdims=True)
    return d * lax.rsqrt(var + LN_EPS) * g + b


def _dot_nt(a, b):
    return lax.dot_general(a, b, (((1,), (1,)), ((), ())), preferred_element_type=F32)


def _conv_chunk(win_ref, base, cw_ref, cb_ref, cg_ref, cbeta_ref):
    off = HALO - CONV_HIST
    tiles = []
    for lt in range(D_CONV // LANES):
        lanes = slice(lt * LANES, (lt + 1) * LANES)
        x = win_ref[pl.ds(base, CONV_ROWS + HALO), lanes]
        acc = None
        for r in range(SUBLANES):
            rows = CONV_ROWS + SUBLANES if r else CONV_ROWS
            part = None
            for a in range((off + CONV_WIDTH - 1 - r) // SUBLANES + 1):
                j = SUBLANES * a + r - off
                if j < 0:
                    continue
                term = cw_ref[j:j + 1, lanes] * x[SUBLANES * a:SUBLANES * a + rows]
                part = term if part is None else part + term
            if r:
                part = pltpu.roll(part, rows - r, 0)[:CONV_ROWS]
            acc = part if acc is None else acc + part
        tiles.append(acc)
    c = jnp.concatenate(tiles, axis=1)
    c = _layer_norm(c + cb_ref[...], cg_ref[...], cbeta_ref[...])
    return c * jax.nn.sigmoid(c)


def _conv_rows_kernel(u_ref, halo_ref, cw_ref, cb_ref, cg_ref, cbeta_ref, c_ref, win_sc):
    tm = u_ref.shape[0]
    win_sc[0:HALO, :] = halo_ref[...]
    win_sc[HALO:HALO + tm, :] = u_ref[...]
    for rc in range(tm // CONV_ROWS):
        base = rc * CONV_ROWS
        c_ref[base:base + CONV_ROWS, :] = _conv_chunk(win_sc, base, cw_ref, cb_ref, cg_ref,
                                                      cbeta_ref).astype(BF16)


def _conv_rows(u, halo, conv_params, layer, tm):
    n = u.shape[0]
    return pl.pallas_call(
        _conv_rows_kernel,
        grid=(n // tm,),
        in_specs=[pl.BlockSpec((tm, D_CONV), lambda i: (i, 0)),
                  pl.BlockSpec((None, HALO, D_CONV), lambda i: (layer, i, 0))]
                 + [_layer_spec(p, layer) for p in conv_params],
        out_specs=pl.BlockSpec((tm, D_CONV), lambda i: (i, 0)),
        out_shape=jax.ShapeDtypeStruct((n, D_CONV), BF16),
        scratch_shapes=[pltpu.VMEM((HALO + tm, D_CONV), F32)],
        compiler_params=_params(1),
        name="conv_rows",
    )(u, halo, *conv_params)


def _rope_rows(p, cos_ref, sin_ref):
    n = p.shape[1] // LANES
    cos = jnp.concatenate([cos_ref[...]] * n, axis=1)
    sin = jnp.concatenate([sin_ref[...]] * n, axis=1)
    lane = lax.broadcasted_iota(jnp.int32, p.shape, 1)
    partner = jnp.where((lane % HEAD_DIM) < HALF,
                        pltpu.roll(p, p.shape[1] - HALF, 1),
                        pltpu.roll(p, HALF, 1))
    return p * cos + partner * sin


def _rope_cols(pt, cos_ref, sin_ref):
    cos = cos_ref[...]
    sin = sin_ref[...]
    parts = []
    for s in range(pt.shape[0] // HEAD_DIM):
        a = pt[s * HEAD_DIM:s * HEAD_DIM + HALF]
        b = pt[s * HEAD_DIM + HALF:(s + 1) * HEAD_DIM]
        parts += [a * cos - b * sin, b * cos + a * sin]
    return jnp.concatenate(parts, axis=0)


def _proj(xb, w_ref, c0, width):
    return jnp.dot(xb, w_ref[:, c0:c0 + width], preferred_element_type=F32)


def _glu(xb, w_ref):
    ca = _proj(xb, w_ref, 3 * D_ATTN, D_CONV)
    cg = _proj(xb, w_ref, 3 * D_ATTN + D_CONV, D_CONV)
    return ca * jax.nn.sigmoid(cg)


def _in_proj_rows_kernel(x_ref, w_ref, cos_ref, sin_ref, q_ref, kf_ref, kb_ref, vf_ref, vb_ref, u_ref):
    xb = x_ref[...].astype(BF16)
    q_ref[...] = (_rope_rows(_proj(xb, w_ref, 0, D_ATTN), cos_ref, sin_ref) * Q_SCALE).astype(BF16)
    k = _rope_rows(_proj(xb, w_ref, D_ATTN, D_ATTN), cos_ref, sin_ref)
    kf_ref[...] = k
    kb_ref[...] = k.astype(BF16)
    v = _proj(xb, w_ref, 2 * D_ATTN, D_ATTN)
    vf_ref[...] = v
    vb_ref[...] = v.astype(BF16)
    u_ref[...] = _glu(xb, w_ref)


def _in_proj_rows(x, w_in, cos, sin, layer, tm):
    n = x.shape[0]
    n_tab = cos.shape[0] // tm
    tok = lambda width: pl.BlockSpec((tm, width), lambda i: (i, 0))
    tab = pl.BlockSpec((tm, LANES), lambda i: (i % n_tab, 0))
    return pl.pallas_call(
        _in_proj_rows_kernel,
        grid=(n // tm,),
        in_specs=[tok(D_MODEL), _layer_spec(w_in, layer), tab, tab],
        out_specs=[tok(D_ATTN)] * 5 + [tok(D_CONV)],
        out_shape=[jax.ShapeDtypeStruct((n, D_ATTN), BF16),
                   jax.ShapeDtypeStruct((n, D_ATTN), F32),
                   jax.ShapeDtypeStruct((n, D_ATTN), BF16),
                   jax.ShapeDtypeStruct((n, D_ATTN), F32),
                   jax.ShapeDtypeStruct((n, D_ATTN), BF16),
                   jax.ShapeDtypeStruct((n, D_CONV), F32)],
        compiler_params=_params(1),
        name="in_proj_rows",
    )(x, w_in, cos, sin)


def _in_proj_conv_kernel(x_ref, w_ref, wkt_ref, cos_ref, sin_ref, cost_ref, sint_ref,
                         cw_ref, cb_ref, cg_ref, cbeta_ref, *rest, tiles_per_seq):
    q_ref, ktf_ref, ktb_ref, vf_ref, vb_ref, c_ref, tail_ref, win_sc = rest[-8:]
    tm = x_ref.shape[0]
    first = (pl.program_id(0) % tiles_per_seq) == 0

    @pl.when(first)
    def _():
        win_sc[0:HALO, :] = jnp.zeros((HALO, D_CONV), F32)

    @pl.when(jnp.logical_not(first))
    def _():
        win_sc[0:HALO, :] = win_sc[tm:tm + HALO, :]

    xb = x_ref[...].astype(BF16)
    u = _glu(xb, w_ref)
    win_sc[HALO:HALO + tm, :] = u
    tail_ref[...] = u[tm - HALO:tm]
    for rc in range(tm // CONV_ROWS):
        base = rc * CONV_ROWS
        c_ref[base:base + CONV_ROWS, :] = _conv_chunk(win_sc, base, cw_ref, cb_ref, cg_ref,
                                                      cbeta_ref).astype(BF16)

    q_ref[...] = (_rope_rows(_proj(xb, w_ref, 0, D_ATTN), cos_ref, sin_ref) * Q_SCALE).astype(BF16)
    kt = _rope_cols(_dot_nt(wkt_ref[...], xb), cost_ref, sint_ref)
    ktf_ref[...] = kt
    ktb_ref[...] = kt.astype(BF16)
    v = _proj(xb, w_ref, 2 * D_ATTN, D_ATTN)
    vb_ref[...] = v.astype(BF16)
    for h in range(N_DIFF_HEADS):
        vf_ref[pl.ds(h, tm, stride=N_DIFF_HEADS), :] = v[:, h * V_DIM:(h + 1) * V_DIM]


def _in_proj_conv(x, w_in, wk_t, cos, sin, cos_t, sin_t, conv_params, layer, batch, seq, tm, prev):
    n = x.shape[0]
    nt = seq // tm
    depth = w_in.shape[0]
    tok = lambda width: pl.BlockSpec((tm, width), lambda i: (i, 0))
    tab = pl.BlockSpec((tm, LANES), lambda i: (i % nt, 0))
    tab_t = pl.BlockSpec((HALF, tm), lambda i: (0, i % nt))
    in_specs = ([tok(D_MODEL), _layer_spec(w_in, layer), _layer_spec(wk_t, layer), tab, tab, tab_t, tab_t]
                + [_layer_spec(p, layer) for p in conv_params])
    args = [x, w_in, wk_t, cos, sin, cos_t, sin_t, *conv_params]
    aliases = {}
    if prev is not None:
        in_specs += [pl.BlockSpec(memory_space=pl.ANY)] * 2
        aliases = {len(args): 1, len(args) + 1: 3}
        args += list(prev)
    out_specs = [tok(D_ATTN),
                 pl.BlockSpec((None, None, D_ATTN, tm), lambda i: (layer, i // nt, 0, i % nt)),
                 pl.BlockSpec((None, D_ATTN, tm), lambda i: (i // nt, 0, i % nt)),
                 pl.BlockSpec((None, tm * N_DIFF_HEADS, V_DIM), lambda i: (layer, i, 0)),
                 tok(D_ATTN), tok(D_CONV),
                 pl.BlockSpec((None, HALO, D_CONV), lambda i: (i // nt, 0, 0))]
    out_shape = [jax.ShapeDtypeStruct((n, D_ATTN), BF16),
                 jax.ShapeDtypeStruct((depth, batch, D_ATTN, seq), F32),
                 jax.ShapeDtypeStruct((batch, D_ATTN, seq), BF16),
                 jax.ShapeDtypeStruct((depth, n * N_DIFF_HEADS, V_DIM), F32),
                 jax.ShapeDtypeStruct((n, D_ATTN), BF16),
                 jax.ShapeDtypeStruct((n, D_CONV), BF16),
                 jax.ShapeDtypeStruct((batch, HALO, D_CONV), F32)]
    return pl.pallas_call(
        functools.partial(_in_proj_conv_kernel, tiles_per_seq=nt),
        grid=(n // tm,),
        in_specs=in_specs,
        out_specs=out_specs,
        out_shape=out_shape,
        input_output_aliases=aliases,
        scratch_shapes=[pltpu.VMEM((HALO + tm, D_CONV), F32)],
        compiler_params=_params(1),
        name="in_proj_conv",
    )(*args)


def _stack_sub_heads(q):
    lane = lax.broadcasted_iota(jnp.int32, q.shape, 1)
    zero = jnp.zeros_like(q)
    return jnp.concatenate([jnp.where(lane < HEAD_DIM, q, zero),
                            jnp.where(lane >= HEAD_DIM, q, zero)], axis=0)


def _lambda(lq1, lk1, lq2, lk2, lam_init):
    s1 = jnp.sum(lq1[...] * lk1[...], axis=1, keepdims=True)
    s2 = jnp.sum(lq2[...] * lk2[...], axis=1, keepdims=True)
    return jnp.exp(s1) - jnp.exp(s2) + lam_init


def _diff_finish(o_all, lam, g, lam_init):
    t = o_all.shape[0] // 2
    o = o_all[:t] - lam * o_all[t:]
    o = o * lax.rsqrt(jnp.mean(o * o, axis=-1, keepdims=True) + RMS_EPS)
    return o * g * (1.0 - lam_init)


def _row_max(s):
    m = s[:, 0:LANES]
    for c in range(1, s.shape[1] // LANES):
        m = jnp.maximum(m, s[:, c * LANES:(c + 1) * LANES])
    return jnp.broadcast_to(jnp.max(m, axis=1, keepdims=True), m.shape)


def _with_ones(v):
    return jnp.concatenate([v, jnp.ones(v.shape, v.dtype)], axis=1)


def _wide(a, width):
    return jnp.concatenate([a] * (width // LANES), axis=1)


def _attn_prompt_kernel(lq1, lk1, lq2, lk2, g_ref, q_ref, kt_ref, v_ref, o_ref, *scratch,
                        lam_init, tq):
    m_scs = scratch[:N_DIFF_HEADS]
    acc_scs = scratch[N_DIFF_HEADS:]
    qi = pl.program_id(1)
    heads = [slice(h * V_DIM, (h + 1) * V_DIM) for h in range(N_DIFF_HEADS)]
    qqs = [_stack_sub_heads(q_ref[:, cols]) for cols in heads]

    def step(start, width, diagonal):
        if diagonal:
            shape = (2 * tq, width)
            q_chunk = (lax.broadcasted_iota(jnp.int32, shape, 0) % tq) // CHUNK
            k_chunk = lax.broadcasted_iota(jnp.int32, shape, 1) // CHUNK
            visible = k_chunk <= q_chunk
        for cols, qq, m_sc, acc_sc in zip(heads, qqs, m_scs, acc_scs):
            kt = kt_ref[cols, pl.ds(start, width)]
            v = v_ref[pl.ds(start, width), cols]
            s = jnp.dot(qq, kt, preferred_element_type=F32)
            if diagonal:
                s = jnp.where(visible, s, NEG_INF)
                m_new = _row_max(s)
            else:
                m_prev = m_sc[...]
                m_new = jnp.maximum(m_prev, _row_max(s))
            p = jnp.exp2(s - _wide(m_new, width)).astype(BF16)
            pv = jnp.dot(p, _with_ones(v), preferred_element_type=F32)
            if diagonal:
                acc_sc[...] = pv
            else:
                acc_sc[...] = _wide(jnp.exp2(m_prev - m_new), 2 * V_DIM) * acc_sc[...] + pv
            m_sc[...] = m_new

    step(pl.multiple_of(qi * tq, tq), tq, True)

    def body(kb, carry):
        step(pl.multiple_of(kb * 2 * tq, 2 * tq), 2 * tq, False)
        return carry

    lax.fori_loop(0, qi // 2, body, 0)

    @pl.when(qi % 2 == 1)
    def _():
        step(pl.multiple_of((qi - 1) * tq, tq), tq, False)

    lam = _lambda(lq1, lk1, lq2, lk2, lam_init)
    for cols, acc_sc in zip(heads, acc_scs):
        acc = acc_sc[...]
        o_all = acc[:, :V_DIM] / acc[:, V_DIM:]
        o_ref[:, cols] = _diff_finish(o_all, lam, g_ref[...], lam_init).astype(o_ref.dtype)


def _attn_prompt(q, kt, v, attn_params, layer, batch, seq, tq):
    nq = seq // tq
    kt_spec = pl.BlockSpec((None, D_ATTN, seq), lambda b, i: (b, 0, 0))
    v_spec = pl.BlockSpec((seq, D_ATTN), lambda b, i: (b, 0))
    q_spec = pl.BlockSpec((tq, D_ATTN), lambda b, i: (b * nq + i, 0))
    return pl.pallas_call(
        functools.partial(_attn_prompt_kernel, lam_init=_lambda_init(layer), tq=tq),
        grid=(batch, nq),
        in_specs=[_layer_spec(p, layer) for p in attn_params] + [q_spec, kt_spec, v_spec],
        out_specs=q_spec,
        out_shape=jax.ShapeDtypeStruct((batch * seq, D_ATTN), BF16),
        scratch_shapes=([pltpu.VMEM((2 * tq, LANES), F32)] * N_DIFF_HEADS
                        + [pltpu.VMEM((2 * tq, 2 * V_DIM), F32)] * N_DIFF_HEADS),
        compiler_params=_params(2),
        name="attn_prompt",
    )(*attn_params, q, kt, v)


def _attn_sample_kernel(lq1, lk1, lq2, lk2, g_ref, q_ref, kn_ref, vn_ref, ckt_ref, cv_ref, o_ref,
                        *, lam_init, past):
    lam = _lambda(lq1, lk1, lq2, lk2, lam_init)
    for h in range(N_DIFF_HEADS):
        cols = slice(h * V_DIM, (h + 1) * V_DIM)
        qq = _stack_sub_heads(q_ref[:, cols])
        s_past = jnp.dot(qq, ckt_ref[cols, :].astype(BF16), preferred_element_type=F32)
        s_new = _dot_nt(qq, kn_ref[:, cols])
        m = jnp.maximum(_row_max(s_past), jnp.max(s_new, axis=1, keepdims=True))
        p_past = jnp.exp2(s_past - _wide(m, past)).astype(BF16)
        p_new = jnp.exp2(s_new - m[:, :s_new.shape[1]]).astype(BF16)
        v_past = cv_ref[pl.ds(h, past, stride=N_DIFF_HEADS), :].astype(BF16)
        acc = (jnp.dot(p_past, _with_ones(v_past), preferred_element_type=F32)
               + jnp.dot(p_new, _with_ones(vn_ref[:, cols]), preferred_element_type=F32))
        o_all = acc[:, :V_DIM] / acc[:, V_DIM:]
        o_ref[:, cols] = _diff_finish(o_all, lam, g_ref[...], lam_init).astype(o_ref.dtype)


def _attn_sample(q, k, v, cache_kt, cache_v, attn_params, layer, batch, seq, past):
    new_spec = pl.BlockSpec((seq, D_ATTN), lambda b: (b, 0))
    return pl.pallas_call(
        functools.partial(_attn_sample_kernel, lam_init=_lambda_init(layer), past=past),
        grid=(batch,),
        in_specs=[_layer_spec(p, layer) for p in attn_params]
                 + [new_spec, new_spec, new_spec,
                    pl.BlockSpec((None, None, D_ATTN, past), lambda b: (layer, b, 0, 0)),
                    pl.BlockSpec((None, None, past * N_DIFF_HEADS, V_DIM), lambda b: (layer, b, 0, 0))],
        out_specs=new_spec,
        out_shape=jax.ShapeDtypeStruct((batch * seq, D_ATTN), BF16),
        compiler_params=_params(1),
        name="attn_sample",
    )(*attn_params, q, k, v, cache_kt, cache_v)


def _post_kernel(attn_ref, c_ref, x_ref, wo_ref, g1_ref, b1_ref, w1_ref, w2_ref, g2_ref, b2_ref, o_ref):
    y = (jnp.dot(attn_ref[...], wo_ref[0:D_ATTN, :], preferred_element_type=F32)
         + jnp.dot(c_ref[...], wo_ref[D_ATTN:D_ATTN + D_CONV, :], preferred_element_type=F32))
    x1 = _layer_norm(DEEPNORM_ALPHA * x_ref[...] + y, g1_ref[...], b1_ref[...])
    xb = x1.astype(BF16)
    acc = jnp.zeros(x1.shape, F32)
    for c in range(D_FF // FF_CHUNK):
        h = jnp.dot(xb, w1_ref[:, c * FF_CHUNK:(c + 1) * FF_CHUNK], preferred_element_type=F32)
        h = jnp.maximum(h, 0.0)
        h = (h * h).astype(BF16)
        acc = acc + jnp.dot(h, w2_ref[c * FF_CHUNK:(c + 1) * FF_CHUNK, :], preferred_element_type=F32)
    o_ref[...] = _layer_norm(DEEPNORM_ALPHA * x1 + acc, g2_ref[...], b2_ref[...])


def _post(attn, c, x, post_params, layer, tm):
    n = x.shape[0]
    tok = lambda width: pl.BlockSpec((tm, width), lambda i: (i, 0))
    return pl.pallas_call(
        _post_kernel,
        grid=(n // tm,),
        in_specs=[tok(D_ATTN), tok(D_CONV), tok(D_MODEL)] + [_layer_spec(p, layer) for p in post_params],
        out_specs=tok(D_MODEL),
        out_shape=jax.ShapeDtypeStruct((n, D_MODEL), F32),
        compiler_params=_params(1),
        name="post",
    )(attn, c, x, *post_params)


def _rope_angles(pos):
    inv = ROPE_THETA ** (-jnp.arange(HALF, dtype=F32) / HALF)
    ang = pos.astype(F32)[:, None] * inv[None, :]
    return jnp.cos(ang), jnp.sin(ang)


def _rope_tables(pos):
    cos, sin = _rope_angles(pos)
    return (jnp.concatenate([cos, cos, cos, cos], axis=1),
            jnp.concatenate([-sin, sin, -sin, sin], axis=1))


def _rows(a):
    return a.reshape(a.shape[0], 1, a.shape[1])


TM = 512
TQ = 256


def kernel(x_prompt, x_sample, cache_k, cache_v, cache_conv, w_in, lambda_q1, lambda_k1, lambda_q2,
           lambda_k2, subln_g, conv_w, conv_b, conv_ln_g, conv_ln_b, w_out, ln1_g, ln1_b, w_ff1,
           w_ff2, ln2_g, ln2_b):
    B, T, _ = x_prompt.shape
    Bs, Ts, _ = x_sample.shape
    P = cache_k.shape[2]
    depth = w_in.shape[0]

    w_in_b = w_in.astype(BF16)
    wk_t = jnp.swapaxes(w_in_b[:, :, D_ATTN:2 * D_ATTN], 1, 2)
    w_out_b = w_out.astype(BF16)
    w_ff1_b = w_ff1.astype(BF16)
    w_ff2_b = w_ff2.astype(BF16)

    pos_p = jnp.arange(T)
    cos_p, sin_p = _rope_tables(pos_p)
    cos_pt, sin_pt = (a.T for a in _rope_angles(pos_p))
    cos_s, sin_s = _rope_tables(P + jnp.arange(Ts))
    reps = TM // Ts
    cos_s = jnp.tile(cos_s, (reps, 1))
    sin_s = jnp.tile(sin_s, (reps, 1))

    xp = x_prompt.reshape(B * T, D_MODEL)
    xs = x_sample.reshape(Bs * Ts, D_MODEL)
    cache_kt = jnp.transpose(cache_k, (0, 1, 3, 4, 2)).reshape(depth, Bs, D_ATTN, P)
    cache_vr = cache_v.reshape(depth, Bs, P * N_DIFF_HEADS, V_DIM)
    hist_s = jnp.pad(cache_conv, ((0, 0), (0, 0), (HALO - CONV_HIST, 0), (0, 0)))
    hist_s = hist_s.reshape(depth, Bs * HALO, D_CONV)

    attn_params = tuple(_rows(a) for a in (lambda_q1, lambda_k1, lambda_q2, lambda_k2, subln_g))
    conv_params = (conv_w, _rows(conv_b), _rows(conv_ln_g), _rows(conv_ln_b))
    post_params = (w_out_b, _rows(ln1_g), _rows(ln1_b), w_ff1_b, w_ff2_b, _rows(ln2_g), _rows(ln2_b))

    outs = {k: [] for k in ("cp", "ks", "vs", "cs")}
    kv_prompt = None
    for l in range(depth):
        q, ktf, ktb, vf, vb, c, tail = _in_proj_conv(xp, w_in_b, wk_t, cos_p, sin_p, cos_pt, sin_pt,
                                                     conv_params, l, B, T, TM, kv_prompt)
        kv_prompt = (ktf, vf)
        attn = _attn_prompt(q, ktb, vb, attn_params, l, B, T, TQ)
        xp = _post(attn, c, xp, post_params, l, TM)
        outs["cp"].append(tail[:, HALO - CONV_HIST:])

        q, kf, kb, vf_s, vb, u = _in_proj_rows(xs, w_in_b, cos_s, sin_s, l, TM)
        attn = _attn_sample(q, kb, vb, cache_kt, cache_vr, attn_params, l, Bs, Ts, P)
        c = _conv_rows(u, hist_s, conv_params, l, Ts)
        xs = _post(attn, c, xs, post_params, l, TM)
        outs["ks"].append(kf.reshape(Bs, Ts, N_SUB_HEADS, HEAD_DIM))
        outs["vs"].append(vf_s.reshape(Bs, Ts, N_DIFF_HEADS, V_DIM))
        outs["cs"].append(u.reshape(Bs, Ts, D_CONV)[:, Ts - CONV_HIST:])

    ktf, vf = kv_prompt
    new_k_prompt = jnp.transpose(ktf.reshape(depth, B, N_SUB_HEADS, HEAD_DIM, T), (0, 1, 4, 2, 3))
    new_v_prompt = vf.reshape(depth, B, T, N_DIFF_HEADS, V_DIM)
    return (xp.reshape(B, T, D_MODEL), xs.reshape(Bs, Ts, D_MODEL),
            new_k_prompt, new_v_prompt, jnp.stack(outs["cp"]),
            jnp.stack(outs["ks"]), jnp.stack(outs["vs"]), jnp.stack(outs["cs"]))
```

```python
import functools
import math

import jax
import jax.numpy as jnp
from jax import lax
from jax.experimental import pallas as pl
from jax.experimental.pallas import tpu as pltpu

D_MODEL = 1024
D_ATTN = 512
D_CONV = 512
N_DIFF_HEADS = 4
N_SUB_HEADS = 8
HEAD_DIM = 64
HALF = HEAD_DIM // 2
V_DIM = 128
LANES = 128
SUBLANES = 8
CHUNK = 64
CONV_WIDTH = 31
CONV_HIST = CONV_WIDTH - 1
HALO = 32
CONV_ROWS = 64
D_FF = 4096
FF_CHUNK = 1024
ROPE_THETA = 10000.0
LN_EPS = 1e-5
RMS_EPS = 1e-5
DEPTH = 2
DEEPNORM_ALPHA = (2.0 * DEPTH) ** 0.25
NEG_INF = -1e30
LOG2E = math.log2(math.e)
Q_SCALE = HEAD_DIM ** -0.5 * LOG2E
VMEM_LIMIT_BYTES = 56 * 1024 * 1024

BF16 = jnp.bfloat16
F32 = jnp.float32


def _lambda_init(layer):
    return 0.8 - 0.6 * math.exp(-0.3 * layer)


def _params(n_axes):
    return pltpu.CompilerParams(dimension_semantics=("arbitrary",) * n_axes,
                                vmem_limit_bytes=VMEM_LIMIT_BYTES)


def _layer_spec(param, layer):
    rest = param.shape[1:]
    return pl.BlockSpec((None,) + rest, lambda *_: (layer,) + (0,) * len(rest),
                        pipeline_mode=pl.Buffered(1))


def _layer_norm(z, g, b):
    mu = jnp.mean(z, axis=-1, keepdims=True)
    d = z - mu
    var = jnp.mean(d * d, axis=-1, keepdims=True)
    return d * lax.rsqrt(var + LN_EPS) * g + b


def _dot_nt(a, b):
    return lax.dot_general(a, b, (((1,), (1,)), ((), ())), preferred_element_type=F32)


def _conv_chunk(win_ref, base, cw_ref, cb_ref, cg_ref, cbeta_ref):
    tiles = [_conv_lanes(win_ref, base, cw_ref, slice(lt * LANES, (lt + 1) * LANES))
             for lt in range(D_CONV // LANES)]
    return _conv_finish(jnp.concatenate(tiles, axis=1), cb_ref, cg_ref, cbeta_ref)


def _conv_lanes(win_ref, base, cw_ref, lanes):
    off = HALO - CONV_HIST
    x = win_ref[pl.ds(base, CONV_ROWS + HALO), lanes]
    acc = None
    for r in range(SUBLANES):
        rows = CONV_ROWS + SUBLANES if r else CONV_ROWS
        part = None
        for a in range((off + CONV_WIDTH - 1 - r) // SUBLANES + 1):
            j = SUBLANES * a + r - off
            if j < 0:
                continue
            term = cw_ref[j:j + 1, lanes] * x[SUBLANES * a:SUBLANES * a + rows]
            part = term if part is None else part + term
        if r:
            part = pltpu.roll(part, rows - r, 0)[:CONV_ROWS]
        acc = part if acc is None else acc + part
    return acc


def _conv_finish(c, cb_ref, cg_ref, cbeta_ref):
    c = _layer_norm(c + cb_ref[...], cg_ref[...], cbeta_ref[...])
    return c * jax.nn.sigmoid(c)


def _conv_rows_kernel(u_ref, halo_ref, cw_ref, cb_ref, cg_ref, cbeta_ref, c_ref, win_sc):
    tm = u_ref.shape[0]
    win_sc[0:HALO, :] = halo_ref[...]
    win_sc[HALO:HALO + tm, :] = u_ref[...]
    for rc in range(tm // CONV_ROWS):
        base = rc * CONV_ROWS
        c_ref[base:base + CONV_ROWS, :] = _conv_chunk(win_sc, base, cw_ref, cb_ref, cg_ref,
                                                      cbeta_ref).astype(BF16)


def _conv_rows(u, halo, conv_params, layer, tm):
    n = u.shape[0]
    return pl.pallas_call(
        _conv_rows_kernel,
        grid=(n // tm,),
        in_specs=[pl.BlockSpec((tm, D_CONV), lambda i: (i, 0)),
                  pl.BlockSpec((None, HALO, D_CONV), lambda i: (layer, i, 0))]
                 + [_layer_spec(p, layer) for p in conv_params],
        out_specs=pl.BlockSpec((tm, D_CONV), lambda i: (i, 0)),
        out_shape=jax.ShapeDtypeStruct((n, D_CONV), BF16),
        scratch_shapes=[pltpu.VMEM((HALO + tm, D_CONV), F32)],
        compiler_params=_params(1),
        name="conv_rows",
    )(u, halo, *conv_params)


def _rope_rows(p, cos_ref, sin_ref):
    n = p.shape[1] // LANES
    cos = jnp.concatenate([cos_ref[...]] * n, axis=1)
    sin = jnp.concatenate([sin_ref[...]] * n, axis=1)
    lane = lax.broadcasted_iota(jnp.int32, p.shape, 1)
    partner = jnp.where((lane % HEAD_DIM) < HALF,
                        pltpu.roll(p, p.shape[1] - HALF, 1),
                        pltpu.roll(p, HALF, 1))
    return p * cos + partner * sin


def _rope_cols(pt, cos_ref, sin_ref):
    cos = cos_ref[...]
    sin = sin_ref[...]
    parts = []
    for s in range(pt.shape[0] // HEAD_DIM):
        a = pt[s * HEAD_DIM:s * HEAD_DIM + HALF]
        b = pt[s * HEAD_DIM + HALF:(s + 1) * HEAD_DIM]
        parts += [a * cos - b * sin, b * cos + a * sin]
    return jnp.concatenate(parts, axis=0)


def _proj(xb, w_ref, c0, width):
    return jnp.dot(xb, w_ref[:, c0:c0 + width], preferred_element_type=F32)


def _glu_lanes(xb, w_ref, lt):
    r = _proj(xb, w_ref, 3 * D_ATTN + 2 * LANES * lt, 2 * LANES)
    return r[:, :LANES] * jax.nn.sigmoid(r[:, LANES:])


def _glu(xb, w_ref):
    return jnp.concatenate([_glu_lanes(xb, w_ref, lt) for lt in range(D_CONV // LANES)], axis=1)


def _in_proj_rows_kernel(x_ref, w_ref, cos_ref, sin_ref, q_ref, kf_ref, kb_ref, vf_ref, vb_ref, u_ref):
    xb = x_ref[...].astype(BF16)
    q_ref[...] = (_rope_rows(_proj(xb, w_ref, 0, D_ATTN), cos_ref, sin_ref) * Q_SCALE).astype(BF16)
    k = _rope_rows(_proj(xb, w_ref, D_ATTN, D_ATTN), cos_ref, sin_ref)
    kf_ref[...] = k
    kb_ref[...] = k.astype(BF16)
    v = _proj(xb, w_ref, 2 * D_ATTN, D_ATTN)
    vf_ref[...] = v
    vb_ref[...] = v.astype(BF16)
    u_ref[...] = _glu(xb, w_ref)


def _in_proj_rows(x, w_in, cos, sin, layer, tm):
    n = x.shape[0]
    n_tab = cos.shape[0] // tm
    tok = lambda width: pl.BlockSpec((tm, width), lambda i: (i, 0))
    tab = pl.BlockSpec((tm, LANES), lambda i: (i % n_tab, 0))
    return pl.pallas_call(
        _in_proj_rows_kernel,
        grid=(n // tm,),
        in_specs=[tok(D_MODEL), _layer_spec(w_in, layer), tab, tab],
        out_specs=[tok(D_ATTN)] * 5 + [tok(D_CONV)],
        out_shape=[jax.ShapeDtypeStruct((n, D_ATTN), BF16),
                   jax.ShapeDtypeStruct((n, D_ATTN), F32),
                   jax.ShapeDtypeStruct((n, D_ATTN), BF16),
                   jax.ShapeDtypeStruct((n, D_ATTN), F32),
                   jax.ShapeDtypeStruct((n, D_ATTN), BF16),
                   jax.ShapeDtypeStruct((n, D_CONV), F32)],
        compiler_params=_params(1),
        name="in_proj_rows",
    )(x, w_in, cos, sin)


def _in_proj_conv_kernel(x_ref, w_ref, wkt_ref, cos_ref, sin_ref, cost_ref, sint_ref,
                         cw_ref, cb_ref, cg_ref, cbeta_ref, *rest, tiles_per_seq):
    q_ref, ktf_ref, ktb_ref, vf_ref, vb_ref, c_ref, tail_ref, win_sc, pre_sc = rest[-9:]
    tm = x_ref.shape[0]
    first = (pl.program_id(0) % tiles_per_seq) == 0

    @pl.when(first)
    def _():
        win_sc[0:HALO, :] = jnp.zeros((HALO, D_CONV), F32)

    @pl.when(jnp.logical_not(first))
    def _():
        win_sc[0:HALO, :] = win_sc[tm:tm + HALO, :]

    xb = x_ref[...].astype(BF16)

    def proj_q():
        q_ref[...] = (_rope_rows(_proj(xb, w_ref, 0, D_ATTN), cos_ref, sin_ref) * Q_SCALE).astype(BF16)

    def proj_k():
        kt = _rope_cols(_dot_nt(wkt_ref[...], xb), cost_ref, sint_ref)
        ktf_ref[...] = kt
        ktb_ref[...] = kt.astype(BF16)

    def proj_v():
        v = _proj(xb, w_ref, 2 * D_ATTN, D_ATTN)
        vb_ref[...] = v.astype(BF16)
        for h in range(N_DIFF_HEADS):
            vf_ref[pl.ds(h, tm, stride=N_DIFF_HEADS), :] = v[:, h * V_DIM:(h + 1) * V_DIM]

    projections = {1: (proj_q, proj_k), 2: (proj_v,)}
    for lt in range(D_CONV // LANES):
        lanes = slice(lt * LANES, (lt + 1) * LANES)
        u = _glu_lanes(xb, w_ref, lt)
        win_sc[HALO:HALO + tm, lanes] = u
        tail_ref[:, lanes] = u[tm - HALO:tm]
        for proj in projections.get(lt, ()):
            proj()
        for rc in range(tm // CONV_ROWS):
            base = rc * CONV_ROWS
            pre_sc[base:base + CONV_ROWS, lanes] = _conv_lanes(win_sc, base, cw_ref, lanes)
    for rc in range(tm // CONV_ROWS):
        rows = slice(rc * CONV_ROWS, (rc + 1) * CONV_ROWS)
        c_ref[rows, :] = _conv_finish(pre_sc[rows, :], cb_ref, cg_ref, cbeta_ref).astype(BF16)


def _in_proj_conv(x, w_in, wk_t, cos, sin, cos_t, sin_t, conv_params, layer, batch, seq, tm, prev):
    n = x.shape[0]
    nt = seq // tm
    depth = w_in.shape[0]
    tok = lambda width: pl.BlockSpec((tm, width), lambda i: (i, 0))
    tab = pl.BlockSpec((tm, LANES), lambda i: (i % nt, 0))
    tab_t = pl.BlockSpec((HALF, tm), lambda i: (0, i % nt))
    in_specs = ([tok(D_MODEL), _layer_spec(w_in, layer), _layer_spec(wk_t, layer), tab, tab, tab_t, tab_t]
                + [_layer_spec(p, layer) for p in conv_params])
    args = [x, w_in, wk_t, cos, sin, cos_t, sin_t, *conv_params]
    aliases = {}
    if prev is not None:
        in_specs += [pl.BlockSpec(memory_space=pl.ANY)] * 2
        aliases = {len(args): 1, len(args) + 1: 3}
        args += list(prev)
    out_specs = [tok(D_ATTN),
                 pl.BlockSpec((None, None, D_ATTN, tm), lambda i: (layer, i // nt, 0, i % nt)),
                 pl.BlockSpec((None, D_ATTN, tm), lambda i: (i // nt, 0, i % nt)),
                 pl.BlockSpec((None, tm * N_DIFF_HEADS, V_DIM), lambda i: (layer, i, 0)),
                 tok(D_ATTN), tok(D_CONV),
                 pl.BlockSpec((None, HALO, D_CONV), lambda i: (i // nt, 0, 0))]
    out_shape = [jax.ShapeDtypeStruct((n, D_ATTN), BF16),
                 jax.ShapeDtypeStruct((depth, batch, D_ATTN, seq), F32),
                 jax.ShapeDtypeStruct((batch, D_ATTN, seq), BF16),
                 jax.ShapeDtypeStruct((depth, n * N_DIFF_HEADS, V_DIM), F32),
                 jax.ShapeDtypeStruct((n, D_ATTN), BF16),
                 jax.ShapeDtypeStruct((n, D_CONV), BF16),
                 jax.ShapeDtypeStruct((batch, HALO, D_CONV), F32)]
    return pl.pallas_call(
        functools.partial(_in_proj_conv_kernel, tiles_per_seq=nt),
        grid=(n // tm,),
        in_specs=in_specs,
        out_specs=out_specs,
        out_shape=out_shape,
        input_output_aliases=aliases,
        scratch_shapes=[pltpu.VMEM((HALO + tm, D_CONV), F32), pltpu.VMEM((tm, D_CONV), F32)],
        compiler_params=_params(1),
        name="in_proj_conv",
    )(*args)


def _stack_sub_heads(q):
    lane = lax.broadcasted_iota(jnp.int32, q.shape, 1)
    zero = jnp.zeros_like(q)
    return jnp.concatenate([jnp.where(lane < HEAD_DIM, q, zero),
                            jnp.where(lane >= HEAD_DIM, q, zero)], axis=0)


def _lambda(lq1, lk1, lq2, lk2, lam_init):
    s1 = jnp.sum(lq1[...] * lk1[...], axis=1, keepdims=True)
    s2 = jnp.sum(lq2[...] * lk2[...], axis=1, keepdims=True)
    return jnp.exp(s1) - jnp.exp(s2) + lam_init


def _diff_finish(o_all, lam, g, lam_init):
    t = o_all.shape[0] // 2
    o = o_all[:t] - lam * o_all[t:]
    o = o * lax.rsqrt(jnp.mean(o * o, axis=-1, keepdims=True) + RMS_EPS)
    return o * g * (1.0 - lam_init)


def _row_max(s):
    m = s[:, 0:LANES]
    for c in range(1, s.shape[1] // LANES):
        m = jnp.maximum(m, s[:, c * LANES:(c + 1) * LANES])
    return jnp.broadcast_to(jnp.max(m, axis=1, keepdims=True), m.shape)


def _with_ones(v):
    return jnp.concatenate([v, jnp.ones(v.shape, v.dtype)], axis=1)


def _wide(a, width):
    return jnp.concatenate([a] * (width // LANES), axis=1)


def _attn_prompt_kernel(lq1, lk1, lq2, lk2, g_ref, q_ref, kt_ref, v_ref, o_ref, *scratch,
                        lam_init, tq):
    m_scs = scratch[:N_DIFF_HEADS]
    acc_scs = scratch[N_DIFF_HEADS:]
    qi = pl.program_id(1)
    heads = [slice(h * V_DIM, (h + 1) * V_DIM) for h in range(N_DIFF_HEADS)]
    qqs = [_stack_sub_heads(q_ref[:, cols]) for cols in heads]

    def step(start, width, diagonal):
        if diagonal:
            shape = (2 * tq, width)
            q_chunk = (lax.broadcasted_iota(jnp.int32, shape, 0) % tq) // CHUNK
            k_chunk = lax.broadcasted_iota(jnp.int32, shape, 1) // CHUNK
            visible = k_chunk <= q_chunk
        for cols, qq, m_sc, acc_sc in zip(heads, qqs, m_scs, acc_scs):
            kt = kt_ref[cols, pl.ds(start, width)]
            v = v_ref[pl.ds(start, width), cols]
            s = jnp.dot(qq, kt, preferred_element_type=F32)
            if diagonal:
                s = jnp.where(visible, s, NEG_INF)
                m_new = _row_max(s)
            else:
                m_prev = m_sc[...]
                m_new = jnp.maximum(m_prev, _row_max(s))
            p = jnp.exp2(s - _wide(m_new, width)).astype(BF16)
            pv = jnp.dot(p, _with_ones(v), preferred_element_type=F32)
            if diagonal:
                acc_sc[...] = pv
            else:
                acc_sc[...] = _wide(jnp.exp2(m_prev - m_new), 2 * V_DIM) * acc_sc[...] + pv
            m_sc[...] = m_new

    step(pl.multiple_of(qi * tq, tq), tq, True)

    def body(kb, carry):
        step(pl.multiple_of(kb * 2 * tq, 2 * tq), 2 * tq, False)
        return carry

    lax.fori_loop(0, qi // 2, body, 0)

    @pl.when(qi % 2 == 1)
    def _():
        step(pl.multiple_of((qi - 1) * tq, tq), tq, False)

    lam = _lambda(lq1, lk1, lq2, lk2, lam_init)
    for cols, acc_sc in zip(heads, acc_scs):
        acc = acc_sc[...]
        o_all = acc[:, :V_DIM] / acc[:, V_DIM:]
        o_ref[:, cols] = _diff_finish(o_all, lam, g_ref[...], lam_init).astype(o_ref.dtype)


def _attn_prompt(q, kt, v, attn_params, layer, batch, seq, tq):
    nq = seq // tq
    kt_spec = pl.BlockSpec((None, D_ATTN, seq), lambda b, i: (b, 0, 0))
    v_spec = pl.BlockSpec((seq, D_ATTN), lambda b, i: (b, 0))
    q_spec = pl.BlockSpec((tq, D_ATTN), lambda b, i: (b * nq + i, 0))
    return pl.pallas_call(
        functools.partial(_attn_prompt_kernel, lam_init=_lambda_init(layer), tq=tq),
        grid=(batch, nq),
        in_specs=[_layer_spec(p, layer) for p in attn_params] + [q_spec, kt_spec, v_spec],
        out_specs=q_spec,
        out_shape=jax.ShapeDtypeStruct((batch * seq, D_ATTN), BF16),
        scratch_shapes=([pltpu.VMEM((2 * tq, LANES), F32)] * N_DIFF_HEADS
                        + [pltpu.VMEM((2 * tq, 2 * V_DIM), F32)] * N_DIFF_HEADS),
        compiler_params=_params(2),
        name="attn_prompt",
    )(*attn_params, q, kt, v)


def _attn_sample_kernel(lq1, lk1, lq2, lk2, g_ref, q_ref, kn_ref, vn_ref, ckt_ref, cv_ref, o_ref,
                        *, lam_init, past):
    lam = _lambda(lq1, lk1, lq2, lk2, lam_init)
    for h in range(N_DIFF_HEADS):
        cols = slice(h * V_DIM, (h + 1) * V_DIM)
        qq = _stack_sub_heads(q_ref[:, cols])
        s_past = jnp.dot(qq, ckt_ref[cols, :].astype(BF16), preferred_element_type=F32)
        s_new = _dot_nt(qq, kn_ref[:, cols])
        m = jnp.maximum(_row_max(s_past), jnp.max(s_new, axis=1, keepdims=True))
        p_past = jnp.exp2(s_past - _wide(m, past)).astype(BF16)
        p_new = jnp.exp2(s_new - m[:, :s_new.shape[1]]).astype(BF16)
        v_past = cv_ref[pl.ds(h, past, stride=N_DIFF_HEADS), :].astype(BF16)
        acc = (jnp.dot(p_past, _with_ones(v_past), preferred_element_type=F32)
               + jnp.dot(p_new, _with_ones(vn_ref[:, cols]), preferred_element_type=F32))
        o_all = acc[:, :V_DIM] / acc[:, V_DIM:]
        o_ref[:, cols] = _diff_finish(o_all, lam, g_ref[...], lam_init).astype(o_ref.dtype)


def _attn_sample(q, k, v, cache_kt, cache_v, attn_params, layer, batch, seq, past):
    new_spec = pl.BlockSpec((seq, D_ATTN), lambda b: (b, 0))
    return pl.pallas_call(
        functools.partial(_attn_sample_kernel, lam_init=_lambda_init(layer), past=past),
        grid=(batch,),
        in_specs=[_layer_spec(p, layer) for p in attn_params]
                 + [new_spec, new_spec, new_spec,
                    pl.BlockSpec((None, None, D_ATTN, past), lambda b: (layer, b, 0, 0)),
                    pl.BlockSpec((None, None, past * N_DIFF_HEADS, V_DIM), lambda b: (layer, b, 0, 0))],
        out_specs=new_spec,
        out_shape=jax.ShapeDtypeStruct((batch * seq, D_ATTN), BF16),
        compiler_params=_params(1),
        name="attn_sample",
    )(*attn_params, q, k, v, cache_kt, cache_v)


def _post_kernel(attn_ref, c_ref, x_ref, wo_ref, g1_ref, b1_ref, w1_ref, w2_ref, g2_ref, b2_ref, o_ref):
    y = (jnp.dot(attn_ref[...], wo_ref[0:D_ATTN, :], preferred_element_type=F32)
         + jnp.dot(c_ref[...], wo_ref[D_ATTN:D_ATTN + D_CONV, :], preferred_element_type=F32))
    x1 = _layer_norm(DEEPNORM_ALPHA * x_ref[...] + y, g1_ref[...], b1_ref[...])
    xb = x1.astype(BF16)
    acc = jnp.zeros(x1.shape, F32)
    for c in range(D_FF // FF_CHUNK):
        h = jnp.dot(xb, w1_ref[:, c * FF_CHUNK:(c + 1) * FF_CHUNK], preferred_element_type=F32)
        h = jnp.maximum(h, 0.0)
        h = (h * h).astype(BF16)
        acc = acc + jnp.dot(h, w2_ref[c * FF_CHUNK:(c + 1) * FF_CHUNK, :], preferred_element_type=F32)
    o_ref[...] = _layer_norm(DEEPNORM_ALPHA * x1 + acc, g2_ref[...], b2_ref[...])


def _post(attn, c, x, post_params, layer, tm):
    n = x.shape[0]
    tok = lambda width: pl.BlockSpec((tm, width), lambda i: (i, 0))
    return pl.pallas_call(
        _post_kernel,
        grid=(n // tm,),
        in_specs=[tok(D_ATTN), tok(D_CONV), tok(D_MODEL)] + [_layer_spec(p, layer) for p in post_params],
        out_specs=tok(D_MODEL),
        out_shape=jax.ShapeDtypeStruct((n, D_MODEL), F32),
        compiler_params=_params(1),
        name="post",
    )(attn, c, x, *post_params)


def _rope_angles(pos):
    inv = ROPE_THETA ** (-jnp.arange(HALF, dtype=F32) / HALF)
    ang = pos.astype(F32)[:, None] * inv[None, :]
    return jnp.cos(ang), jnp.sin(ang)


def _rope_tables(pos):
    cos, sin = _rope_angles(pos)
    return (jnp.concatenate([cos, cos, cos, cos], axis=1),
            jnp.concatenate([-sin, sin, -sin, sin], axis=1))


def _rows(a):
    return a.reshape(a.shape[0], 1, a.shape[1])


TM = 512
TQ = 256


def kernel(x_prompt, x_sample, cache_k, cache_v, cache_conv, w_in, lambda_q1, lambda_k1, lambda_q2,
           lambda_k2, subln_g, conv_w, conv_b, conv_ln_g, conv_ln_b, w_out, ln1_g, ln1_b, w_ff1,
           w_ff2, ln2_g, ln2_b):
    B, T, _ = x_prompt.shape
    Bs, Ts, _ = x_sample.shape
    P = cache_k.shape[2]
    depth = w_in.shape[0]

    w_in_b = w_in.astype(BF16)
    wk_t = jnp.swapaxes(w_in_b[:, :, D_ATTN:2 * D_ATTN], 1, 2)
    n_lt = D_CONV // LANES
    glu = w_in_b[:, :, 3 * D_ATTN:].reshape(depth, D_MODEL, 2, n_lt, LANES)
    glu = jnp.swapaxes(glu, 2, 3).reshape(depth, D_MODEL, 2 * D_CONV)
    w_in_b = jnp.concatenate([w_in_b[:, :, :3 * D_ATTN], glu], axis=2)
    w_out_b = w_out.astype(BF16)
    w_ff1_b = w_ff1.astype(BF16)
    w_ff2_b = w_ff2.astype(BF16)

    pos_p = jnp.arange(T)
    cos_p, sin_p = _rope_tables(pos_p)
    cos_pt, sin_pt = (a.T for a in _rope_angles(pos_p))
    cos_s, sin_s = _rope_tables(P + jnp.arange(Ts))
    reps = TM // Ts
    cos_s = jnp.tile(cos_s, (reps, 1))
    sin_s = jnp.tile(sin_s, (reps, 1))

    xp = x_prompt.reshape(B * T, D_MODEL)
    xs = x_sample.reshape(Bs * Ts, D_MODEL)
    cache_kt = jnp.transpose(cache_k, (0, 1, 3, 4, 2)).reshape(depth, Bs, D_ATTN, P)
    cache_vr = cache_v.reshape(depth, Bs, P * N_DIFF_HEADS, V_DIM)
    hist_s = jnp.pad(cache_conv, ((0, 0), (0, 0), (HALO - CONV_HIST, 0), (0, 0)))
    hist_s = hist_s.reshape(depth, Bs * HALO, D_CONV)

    attn_params = tuple(_rows(a) for a in (lambda_q1, lambda_k1, lambda_q2, lambda_k2, subln_g))
    conv_params = (conv_w, _rows(conv_b), _rows(conv_ln_g), _rows(conv_ln_b))
    post_params = (w_out_b, _rows(ln1_g), _rows(ln1_b), w_ff1_b, w_ff2_b, _rows(ln2_g), _rows(ln2_b))

    outs = {k: [] for k in ("cp", "ks", "vs", "cs")}
    kv_prompt = None
    for l in range(depth):
        q, ktf, ktb, vf, vb, c, tail = _in_proj_conv(xp, w_in_b, wk_t, cos_p, sin_p, cos_pt, sin_pt,
                                                     conv_params, l, B, T, TM, kv_prompt)
        kv_prompt = (ktf, vf)
        attn = _attn_prompt(q, ktb, vb, attn_params, l, B, T, TQ)
        xp = _post(attn, c, xp, post_params, l, TM)
        outs["cp"].append(tail[:, HALO - CONV_HIST:])

        q, kf, kb, vf_s, vb, u = _in_proj_rows(xs, w_in_b, cos_s, sin_s, l, TM)
        attn = _attn_sample(q, kb, vb, cache_kt, cache_vr, attn_params, l, Bs, Ts, P)
        c = _conv_rows(u, hist_s, conv_params, l, Ts)
        xs = _post(attn, c, xs, post_params, l, TM)
        outs["ks"].append(kf.reshape(Bs, Ts, N_SUB_HEADS, HEAD_DIM))
        outs["vs"].append(vf_s.reshape(Bs, Ts, N_DIFF_HEADS, V_DIM))
        outs["cs"].append(u.reshape(Bs, Ts, D_CONV)[:, Ts - CONV_HIST:])

    ktf, vf = kv_prompt
    new_k_prompt = jnp.transpose(ktf.reshape(depth, B, N_SUB_HEADS, HEAD_DIM, T), (0, 1, 4, 2, 3))
    new_v_prompt = vf.reshape(depth, B, T, N_DIFF_HEADS, V_DIM)
    return (xp.reshape(B, T, D_MODEL), xs.reshape(Bs, Ts, D_MODEL),
            new_k_prompt, new_v_prompt, jnp.stack(outs["cp"]),
            jnp.stack(outs["ks"]), jnp.stack(outs["vs"]), jnp.stack(outs["cs"]))
```

```python
import functools
import math

import jax
import jax.numpy as jnp
from jax import lax
from jax.experimental import pallas as pl
from jax.experimental.pallas import tpu as pltpu

D_MODEL = 1024
D_ATTN = 512
D_CONV = 512
N_DIFF_HEADS = 4
N_SUB_HEADS = 8
HEAD_DIM = 64
HALF = HEAD_DIM // 2
V_DIM = 128
LANES = 128
SUBLANES = 8
CHUNK = 64
CONV_WIDTH = 31
CONV_HIST = CONV_WIDTH - 1
HALO = 32
CONV_ROWS = 64
D_FF = 4096
FF_CHUNK = 1024
ROPE_THETA = 10000.0
LN_EPS = 1e-5
RMS_EPS = 1e-5
DEPTH = 2
DEEPNORM_ALPHA = (2.0 * DEPTH) ** 0.25
NEG_INF = -1e30
LOG2E = math.log2(math.e)
Q_SCALE = HEAD_DIM ** -0.5 * LOG2E
VMEM_LIMIT_BYTES = 56 * 1024 * 1024

BF16 = jnp.bfloat16
F32 = jnp.float32


def _lambda_init(layer):
    return 0.8 - 0.6 * math.exp(-0.3 * layer)


def _params(n_axes):
    return pltpu.CompilerParams(dimension_semantics=("arbitrary",) * n_axes,
                                vmem_limit_bytes=VMEM_LIMIT_BYTES)


def _layer_spec(param, layer):
    rest = param.shape[1:]
    return pl.BlockSpec((None,) + rest, lambda *_: (layer,) + (0,) * len(rest),
                        pipeline_mode=pl.Buffered(1))


def _layer_norm(z, g, b):
    mu = jnp.mean(z, axis=-1, keepdims=True)
    d = z - mu
    var = jnp.mean(d * d, axis=-1, keepdims=True)
    return d * lax.rsqrt(var + LN_EPS) * g + b


def _dot_nt(a, b):
    return lax.dot_general(a, b, (((1,), (1,)), ((), ())), preferred_element_type=F32)


def _conv_chunk(win_ref, base, cw_ref, cb_ref, cg_ref, cbeta_ref):
    tiles = [_conv_lanes(win_ref, base, cw_ref, slice(lt * LANES, (lt + 1) * LANES))
             for lt in range(D_CONV // LANES)]
    return _conv_finish(jnp.concatenate(tiles, axis=1), cb_ref, cg_ref, cbeta_ref)


def _conv_lanes(win_ref, base, cw_ref, lanes):
    off = HALO - CONV_HIST
    x = win_ref[pl.ds(base, CONV_ROWS + HALO), lanes]
    acc = None
    for r in range(SUBLANES):
        rows = CONV_ROWS + SUBLANES if r else CONV_ROWS
        part = None
        for a in range((off + CONV_WIDTH - 1 - r) // SUBLANES + 1):
            j = SUBLANES * a + r - off
            if j < 0:
                continue
            term = cw_ref[j:j + 1, lanes] * x[SUBLANES * a:SUBLANES * a + rows]
            part = term if part is None else part + term
        if r:
            part = pltpu.roll(part, rows - r, 0)[:CONV_ROWS]
        acc = part if acc is None else acc + part
    return acc


def _conv_finish(c, cb_ref, cg_ref, cbeta_ref):
    c = _layer_norm(c + cb_ref[...], cg_ref[...], cbeta_ref[...])
    return c * jax.nn.sigmoid(c)


def _conv_rows_kernel(u_ref, halo_ref, cw_ref, cb_ref, cg_ref, cbeta_ref, c_ref, win_sc):
    tm = u_ref.shape[0]
    win_sc[0:HALO, :] = halo_ref[...]
    win_sc[HALO:HALO + tm, :] = u_ref[...]
    for rc in range(tm // CONV_ROWS):
        base = rc * CONV_ROWS
        c_ref[base:base + CONV_ROWS, :] = _conv_chunk(win_sc, base, cw_ref, cb_ref, cg_ref,
                                                      cbeta_ref).astype(BF16)


def _conv_rows(u, halo, conv_params, layer, tm):
    n = u.shape[0]
    return pl.pallas_call(
        _conv_rows_kernel,
        grid=(n // tm,),
        in_specs=[pl.BlockSpec((tm, D_CONV), lambda i: (i, 0)),
                  pl.BlockSpec((None, HALO, D_CONV), lambda i: (layer, i, 0))]
                 + [_layer_spec(p, layer) for p in conv_params],
        out_specs=pl.BlockSpec((tm, D_CONV), lambda i: (i, 0)),
        out_shape=jax.ShapeDtypeStruct((n, D_CONV), BF16),
        scratch_shapes=[pltpu.VMEM((HALO + tm, D_CONV), F32)],
        compiler_params=_params(1),
        name="conv_rows",
    )(u, halo, *conv_params)


def _rope_rows(p, cos_ref, sin_ref):
    n = p.shape[1] // LANES
    cos = jnp.concatenate([cos_ref[...]] * n, axis=1)
    sin = jnp.concatenate([sin_ref[...]] * n, axis=1)
    lane = lax.broadcasted_iota(jnp.int32, p.shape, 1)
    partner = jnp.where((lane % HEAD_DIM) < HALF,
                        pltpu.roll(p, p.shape[1] - HALF, 1),
                        pltpu.roll(p, HALF, 1))
    return p * cos + partner * sin


def _rope_cols(pt, cos_ref, sin_ref):
    cos = cos_ref[...]
    sin = sin_ref[...]
    parts = []
    for s in range(pt.shape[0] // HEAD_DIM):
        a = pt[s * HEAD_DIM:s * HEAD_DIM + HALF]
        b = pt[s * HEAD_DIM + HALF:(s + 1) * HEAD_DIM]
        parts += [a * cos - b * sin, b * cos + a * sin]
    return jnp.concatenate(parts, axis=0)


def _proj(xb, w_ref, c0, width):
    return jnp.dot(xb, w_ref[:, c0:c0 + width], preferred_element_type=F32)


def _glu_lanes(xb, w_ref, lt):
    r = _proj(xb, w_ref, 3 * D_ATTN + 2 * LANES * lt, 2 * LANES)
    return r[:, :LANES] * jax.nn.sigmoid(r[:, LANES:])


def _glu(xb, w_ref):
    return jnp.concatenate([_glu_lanes(xb, w_ref, lt) for lt in range(D_CONV // LANES)], axis=1)


def _in_proj_rows_kernel(x_ref, w_ref, cos_ref, sin_ref, q_ref, kf_ref, kb_ref, vf_ref, vb_ref, u_ref):
    xb = x_ref[...].astype(BF16)
    q_ref[...] = (_rope_rows(_proj(xb, w_ref, 0, D_ATTN), cos_ref, sin_ref) * Q_SCALE).astype(BF16)
    k = _rope_rows(_proj(xb, w_ref, D_ATTN, D_ATTN), cos_ref, sin_ref)
    kf_ref[...] = k
    kb_ref[...] = k.astype(BF16)
    v = _proj(xb, w_ref, 2 * D_ATTN, D_ATTN)
    vf_ref[...] = v
    vb_ref[...] = v.astype(BF16)
    u_ref[...] = _glu(xb, w_ref)


def _in_proj_rows(x, w_in, cos, sin, layer, tm):
    n = x.shape[0]
    n_tab = cos.shape[0] // tm
    tok = lambda width: pl.BlockSpec((tm, width), lambda i: (i, 0))
    tab = pl.BlockSpec((tm, LANES), lambda i: (i % n_tab, 0))
    return pl.pallas_call(
        _in_proj_rows_kernel,
        grid=(n // tm,),
        in_specs=[tok(D_MODEL), _layer_spec(w_in, layer), tab, tab],
        out_specs=[tok(D_ATTN)] * 5 + [tok(D_CONV)],
        out_shape=[jax.ShapeDtypeStruct((n, D_ATTN), BF16),
                   jax.ShapeDtypeStruct((n, D_ATTN), F32),
                   jax.ShapeDtypeStruct((n, D_ATTN), BF16),
                   jax.ShapeDtypeStruct((n, D_ATTN), F32),
                   jax.ShapeDtypeStruct((n, D_ATTN), BF16),
                   jax.ShapeDtypeStruct((n, D_CONV), F32)],
        compiler_params=_params(1),
        name="in_proj_rows",
    )(x, w_in, cos, sin)


def _in_proj_conv_kernel(x_ref, w_ref, wkt_ref, cos_ref, sin_ref, cost_ref, sint_ref,
                         cw_ref, cb_ref, cg_ref, cbeta_ref, *rest, tiles_per_seq):
    q_ref, ktf_ref, ktb_ref, vf_ref, vb_ref, c_ref, tail_ref, win_sc, pre_sc = rest[-9:]
    tm = x_ref.shape[0]
    first = (pl.program_id(0) % tiles_per_seq) == 0

    @pl.when(first)
    def _():
        win_sc[0:HALO, :] = jnp.zeros((HALO, D_CONV), F32)

    @pl.when(jnp.logical_not(first))
    def _():
        win_sc[0:HALO, :] = win_sc[tm:tm + HALO, :]

    xb = x_ref[...].astype(BF16)

    def proj_q():
        q_ref[...] = (_rope_rows(_proj(xb, w_ref, 0, D_ATTN), cos_ref, sin_ref) * Q_SCALE).astype(BF16)

    def proj_k():
        kt = _rope_cols(_dot_nt(wkt_ref[...], xb), cost_ref, sint_ref)
        ktf_ref[...] = kt
        ktb_ref[...] = kt.astype(BF16)

    def proj_v():
        v = _proj(xb, w_ref, 2 * D_ATTN, D_ATTN)
        vb_ref[...] = v.astype(BF16)
        for h in range(N_DIFF_HEADS):
            vf_ref[pl.ds(h, tm, stride=N_DIFF_HEADS), :] = v[:, h * V_DIM:(h + 1) * V_DIM]

    projections = {1: (proj_q, proj_k), 2: (proj_v,)}
    for lt in range(D_CONV // LANES):
        lanes = slice(lt * LANES, (lt + 1) * LANES)
        u = _glu_lanes(xb, w_ref, lt)
        win_sc[HALO:HALO + tm, lanes] = u
        tail_ref[:, lanes] = u[tm - HALO:tm]
        for proj in projections.get(lt, ()):
            proj()
        for rc in range(tm // CONV_ROWS):
            base = rc * CONV_ROWS
            pre_sc[base:base + CONV_ROWS, lanes] = _conv_lanes(win_sc, base, cw_ref, lanes)
    for rc in range(tm // CONV_ROWS):
        rows = slice(rc * CONV_ROWS, (rc + 1) * CONV_ROWS)
        c_ref[rows, :] = _conv_finish(pre_sc[rows, :], cb_ref, cg_ref, cbeta_ref).astype(BF16)


def _in_proj_conv(x, w_in, wk_t, cos, sin, cos_t, sin_t, conv_params, layer, batch, seq, tm, prev):
    n = x.shape[0]
    nt = seq // tm
    depth = w_in.shape[0]
    tok = lambda width: pl.BlockSpec((tm, width), lambda i: (i, 0))
    tab = pl.BlockSpec((tm, LANES), lambda i: (i % nt, 0))
    tab_t = pl.BlockSpec((HALF, tm), lambda i: (0, i % nt))
    in_specs = ([tok(D_MODEL), _layer_spec(w_in, layer), _layer_spec(wk_t, layer), tab, tab, tab_t, tab_t]
                + [_layer_spec(p, layer) for p in conv_params])
    args = [x, w_in, wk_t, cos, sin, cos_t, sin_t, *conv_params]
    aliases = {}
    if prev is not None:
        in_specs += [pl.BlockSpec(memory_space=pl.ANY)] * 2
        aliases = {len(args): 1, len(args) + 1: 3}
        args += list(prev)
    out_specs = [tok(D_ATTN),
                 pl.BlockSpec((None, None, D_ATTN, tm), lambda i: (layer, i // nt, 0, i % nt)),
                 pl.BlockSpec((None, D_ATTN, tm), lambda i: (i // nt, 0, i % nt)),
                 pl.BlockSpec((None, tm * N_DIFF_HEADS, V_DIM), lambda i: (layer, i, 0)),
                 tok(D_ATTN), tok(D_CONV),
                 pl.BlockSpec((None, HALO, D_CONV), lambda i: (i // nt, 0, 0))]
    out_shape = [jax.ShapeDtypeStruct((n, D_ATTN), BF16),
                 jax.ShapeDtypeStruct((depth, batch, D_ATTN, seq), F32),
                 jax.ShapeDtypeStruct((batch, D_ATTN, seq), BF16),
                 jax.ShapeDtypeStruct((depth, n * N_DIFF_HEADS, V_DIM), F32),
                 jax.ShapeDtypeStruct((n, D_ATTN), BF16),
                 jax.ShapeDtypeStruct((n, D_CONV), BF16),
                 jax.ShapeDtypeStruct((batch, HALO, D_CONV), F32)]
    return pl.pallas_call(
        functools.partial(_in_proj_conv_kernel, tiles_per_seq=nt),
        grid=(n // tm,),
        in_specs=in_specs,
        out_specs=out_specs,
        out_shape=out_shape,
        input_output_aliases=aliases,
        scratch_shapes=[pltpu.VMEM((HALO + tm, D_CONV), F32), pltpu.VMEM((tm, D_CONV), F32)],
        compiler_params=_params(1),
        name="in_proj_conv",
    )(*args)


def _stack_sub_heads(q):
    lane = lax.broadcasted_iota(jnp.int32, q.shape, 1)
    zero = jnp.zeros_like(q)
    return jnp.concatenate([jnp.where(lane < HEAD_DIM, q, zero),
                            jnp.where(lane >= HEAD_DIM, q, zero)], axis=0)


def _lambda(lq1, lk1, lq2, lk2, lam_init):
    s1 = jnp.sum(lq1[...] * lk1[...], axis=1, keepdims=True)
    s2 = jnp.sum(lq2[...] * lk2[...], axis=1, keepdims=True)
    return jnp.exp(s1) - jnp.exp(s2) + lam_init


def _diff_finish(o_all, lam, g, lam_init):
    t = o_all.shape[0] // 2
    o = o_all[:t] - lam * o_all[t:]
    o = o * lax.rsqrt(jnp.mean(o * o, axis=-1, keepdims=True) + RMS_EPS)
    return o * g * (1.0 - lam_init)


def _row_max(s):
    m = s[:, 0:LANES]
    for c in range(1, s.shape[1] // LANES):
        m = jnp.maximum(m, s[:, c * LANES:(c + 1) * LANES])
    return jnp.broadcast_to(jnp.max(m, axis=1, keepdims=True), m.shape)


def _with_ones(v):
    return jnp.concatenate([v, jnp.ones(v.shape, v.dtype)], axis=1)


def _wide(a, width):
    return jnp.concatenate([a] * (width // LANES), axis=1)


def _attn_prompt_kernel(lq1, lk1, lq2, lk2, g_ref, q_ref, kt_ref, v_ref, o_ref, *scratch,
                        lam_init, tq):
    m_scs = scratch[:N_DIFF_HEADS]
    acc_scs = scratch[N_DIFF_HEADS:]
    qi = pl.program_id(1)
    heads = [slice(h * V_DIM, (h + 1) * V_DIM) for h in range(N_DIFF_HEADS)]
    qqs = [_stack_sub_heads(q_ref[:, cols]) for cols in heads]

    def step(start, width, diagonal):
        if diagonal:
            shape = (2 * tq, width)
            q_chunk = (lax.broadcasted_iota(jnp.int32, shape, 0) % tq) // CHUNK
            k_chunk = lax.broadcasted_iota(jnp.int32, shape, 1) // CHUNK
            visible = k_chunk <= q_chunk
        scores = [jnp.dot(qq, kt_ref[cols, pl.ds(start, width)], preferred_element_type=F32)
                  for cols, qq in zip(heads, qqs)]
        for cols, s, m_sc, acc_sc in zip(heads, scores, m_scs, acc_scs):
            v = v_ref[pl.ds(start, width), cols]
            if diagonal:
                s = jnp.where(visible, s, NEG_INF)
                m_new = _row_max(s)
            else:
                m_prev = m_sc[...]
                m_new = jnp.maximum(m_prev, _row_max(s))
            p = jnp.exp2(s - _wide(m_new, width)).astype(BF16)
            pv = jnp.dot(p, _with_ones(v), preferred_element_type=F32)
            if diagonal:
                acc_sc[...] = pv
            else:
                acc_sc[...] = _wide(jnp.exp2(m_prev - m_new), 2 * V_DIM) * acc_sc[...] + pv
            m_sc[...] = m_new

    step(pl.multiple_of(qi * tq, tq), tq, True)

    def body(kb, carry):
        step(pl.multiple_of(kb * 4 * tq, 4 * tq), 4 * tq, False)
        return carry

    lax.fori_loop(0, qi // 4, body, 0)

    @pl.when(qi % 4 >= 2)
    def _():
        step(pl.multiple_of((qi // 4) * 4 * tq, 2 * tq), 2 * tq, False)

    @pl.when(qi % 2 == 1)
    def _():
        step(pl.multiple_of((qi - 1) * tq, tq), tq, False)

    lam = _lambda(lq1, lk1, lq2, lk2, lam_init)
    for cols, acc_sc in zip(heads, acc_scs):
        acc = acc_sc[...]
        o_all = acc[:, :V_DIM] / acc[:, V_DIM:]
        o_ref[:, cols] = _diff_finish(o_all, lam, g_ref[...], lam_init).astype(o_ref.dtype)


def _attn_prompt(q, kt, v, attn_params, layer, batch, seq, tq):
    nq = seq // tq
    kt_spec = pl.BlockSpec((None, D_ATTN, seq), lambda b, i: (b, 0, 0))
    v_spec = pl.BlockSpec((seq, D_ATTN), lambda b, i: (b, 0))
    q_spec = pl.BlockSpec((tq, D_ATTN), lambda b, i: (b * nq + i, 0))
    return pl.pallas_call(
        functools.partial(_attn_prompt_kernel, lam_init=_lambda_init(layer), tq=tq),
        grid=(batch, nq),
        in_specs=[_layer_spec(p, layer) for p in attn_params] + [q_spec, kt_spec, v_spec],
        out_specs=q_spec,
        out_shape=jax.ShapeDtypeStruct((batch * seq, D_ATTN), BF16),
        scratch_shapes=([pltpu.VMEM((2 * tq, LANES), F32)] * N_DIFF_HEADS
                        + [pltpu.VMEM((2 * tq, 2 * V_DIM), F32)] * N_DIFF_HEADS),
        compiler_params=_params(2),
        name="attn_prompt",
    )(*attn_params, q, kt, v)


def _attn_sample_kernel(lq1, lk1, lq2, lk2, g_ref, q_ref, kn_ref, vn_ref, ckt_ref, cv_ref, o_ref,
                        *, lam_init, past):
    lam = _lambda(lq1, lk1, lq2, lk2, lam_init)
    for h in range(N_DIFF_HEADS):
        cols = slice(h * V_DIM, (h + 1) * V_DIM)
        qq = _stack_sub_heads(q_ref[:, cols])
        s_past = jnp.dot(qq, ckt_ref[cols, :].astype(BF16), preferred_element_type=F32)
        s_new = _dot_nt(qq, kn_ref[:, cols])
        m = jnp.maximum(_row_max(s_past), jnp.max(s_new, axis=1, keepdims=True))
        p_past = jnp.exp2(s_past - _wide(m, past)).astype(BF16)
        p_new = jnp.exp2(s_new - m[:, :s_new.shape[1]]).astype(BF16)
        v_past = cv_ref[pl.ds(h, past, stride=N_DIFF_HEADS), :].astype(BF16)
        acc = (jnp.dot(p_past, _with_ones(v_past), preferred_element_type=F32)
               + jnp.dot(p_new, _with_ones(vn_ref[:, cols]), preferred_element_type=F32))
        o_all = acc[:, :V_DIM] / acc[:, V_DIM:]
        o_ref[:, cols] = _diff_finish(o_all, lam, g_ref[...], lam_init).astype(o_ref.dtype)


def _attn_sample(q, k, v, cache_kt, cache_v, attn_params, layer, batch, seq, past):
    new_spec = pl.BlockSpec((seq, D_ATTN), lambda b: (b, 0))
    return pl.pallas_call(
        functools.partial(_attn_sample_kernel, lam_init=_lambda_init(layer), past=past),
        grid=(batch,),
        in_specs=[_layer_spec(p, layer) for p in attn_params]
                 + [new_spec, new_spec, new_spec,
                    pl.BlockSpec((None, None, D_ATTN, past), lambda b: (layer, b, 0, 0)),
                    pl.BlockSpec((None, None, past * N_DIFF_HEADS, V_DIM), lambda b: (layer, b, 0, 0))],
        out_specs=new_spec,
        out_shape=jax.ShapeDtypeStruct((batch * seq, D_ATTN), BF16),
        compiler_params=_params(1),
        name="attn_sample",
    )(*attn_params, q, k, v, cache_kt, cache_v)


def _post_kernel(attn_ref, c_ref, x_ref, wo_ref, g1_ref, b1_ref, w1_ref, w2_ref, g2_ref, b2_ref, o_ref):
    y = (jnp.dot(attn_ref[...], wo_ref[0:D_ATTN, :], preferred_element_type=F32)
         + jnp.dot(c_ref[...], wo_ref[D_ATTN:D_ATTN + D_CONV, :], preferred_element_type=F32))
    x1 = _layer_norm(DEEPNORM_ALPHA * x_ref[...] + y, g1_ref[...], b1_ref[...])
    xb = x1.astype(BF16)
    acc = jnp.zeros(x1.shape, F32)
    for c in range(D_FF // FF_CHUNK):
        h = jnp.dot(xb, w1_ref[:, c * FF_CHUNK:(c + 1) * FF_CHUNK], preferred_element_type=F32)
        h = jnp.maximum(h, 0.0)
        h = (h * h).astype(BF16)
        acc = acc + jnp.dot(h, w2_ref[c * FF_CHUNK:(c + 1) * FF_CHUNK, :], preferred_element_type=F32)
    o_ref[...] = _layer_norm(DEEPNORM_ALPHA * x1 + acc, g2_ref[...], b2_ref[...])


def _post(attn, c, x, post_params, layer, tm):
    n = x.shape[0]
    tok = lambda width: pl.BlockSpec((tm, width), lambda i: (i, 0))
    return pl.pallas_call(
        _post_kernel,
        grid=(n // tm,),
        in_specs=[tok(D_ATTN), tok(D_CONV), tok(D_MODEL)] + [_layer_spec(p, layer) for p in post_params],
        out_specs=tok(D_MODEL),
        out_shape=jax.ShapeDtypeStruct((n, D_MODEL), F32),
        compiler_params=_params(1),
        name="post",
    )(attn, c, x, *post_params)


def _rope_angles(pos):
    inv = ROPE_THETA ** (-jnp.arange(HALF, dtype=F32) / HALF)
    ang = pos.astype(F32)[:, None] * inv[None, :]
    return jnp.cos(ang), jnp.sin(ang)


def _rope_tables(pos):
    cos, sin = _rope_angles(pos)
    return (jnp.concatenate([cos, cos, cos, cos], axis=1),
            jnp.concatenate([-sin, sin, -sin, sin], axis=1))


def _rows(a):
    return a.reshape(a.shape[0], 1, a.shape[1])


TM = 512
TQ = 256


def kernel(x_prompt, x_sample, cache_k, cache_v, cache_conv, w_in, lambda_q1, lambda_k1, lambda_q2,
           lambda_k2, subln_g, conv_w, conv_b, conv_ln_g, conv_ln_b, w_out, ln1_g, ln1_b, w_ff1,
           w_ff2, ln2_g, ln2_b):
    B, T, _ = x_prompt.shape
    Bs, Ts, _ = x_sample.shape
    P = cache_k.shape[2]
    depth = w_in.shape[0]

    cols = [w_in[:, :, :3 * D_ATTN]]
    for lt in range(D_CONV // LANES):
        for c0 in (3 * D_ATTN, 3 * D_ATTN + D_CONV):
            cols.append(w_in[:, :, c0 + lt * LANES:c0 + (lt + 1) * LANES])
    w_in_b = jnp.concatenate(cols, axis=2).astype(BF16)
    wk_t = jnp.swapaxes(w_in_b[:, :, D_ATTN:2 * D_ATTN], 1, 2)
    w_out_b = w_out.astype(BF16)
    w_ff1_b = w_ff1.astype(BF16)
    w_ff2_b = w_ff2.astype(BF16)

    pos_p = jnp.arange(T)
    cos_p, sin_p = _rope_tables(pos_p)
    cos_pt, sin_pt = (a.T for a in _rope_angles(pos_p))
    cos_s, sin_s = _rope_tables(P + jnp.arange(Ts))
    reps = TM // Ts
    cos_s = jnp.tile(cos_s, (reps, 1))
    sin_s = jnp.tile(sin_s, (reps, 1))

    xp = x_prompt.reshape(B * T, D_MODEL)
    xs = x_sample.reshape(Bs * Ts, D_MODEL)
    cache_kt = jnp.transpose(cache_k, (0, 1, 3, 4, 2)).reshape(depth, Bs, D_ATTN, P)
    cache_vr = cache_v.reshape(depth, Bs, P * N_DIFF_HEADS, V_DIM)
    hist_s = jnp.pad(cache_conv, ((0, 0), (0, 0), (HALO - CONV_HIST, 0), (0, 0)))
    hist_s = hist_s.reshape(depth, Bs * HALO, D_CONV)

    attn_params = tuple(_rows(a) for a in (lambda_q1, lambda_k1, lambda_q2, lambda_k2, subln_g))
    conv_params = (conv_w, _rows(conv_b), _rows(conv_ln_g), _rows(conv_ln_b))
    post_params = (w_out_b, _rows(ln1_g), _rows(ln1_b), w_ff1_b, w_ff2_b, _rows(ln2_g), _rows(ln2_b))

    outs = {k: [] for k in ("cp", "ks", "vs", "cs")}
    kv_prompt = None
    for l in range(depth):
        q, ktf, ktb, vf, vb, c, tail = _in_proj_conv(xp, w_in_b, wk_t, cos_p, sin_p, cos_pt, sin_pt,
                                                     conv_params, l, B, T, TM, kv_prompt)
        kv_prompt = (ktf, vf)
        attn = _attn_prompt(q, ktb, vb, attn_params, l, B, T, TQ)
        xp = _post(attn, c, xp, post_params, l, TM)
        outs["cp"].append(tail[:, HALO - CONV_HIST:])

        q, kf, kb, vf_s, vb, u = _in_proj_rows(xs, w_in_b, cos_s, sin_s, l, TM)
        attn = _attn_sample(q, kb, vb, cache_kt, cache_vr, attn_params, l, Bs, Ts, P)
        c = _conv_rows(u, hist_s, conv_params, l, Ts)
        xs = _post(attn, c, xs, post_params, l, TM)
        outs["ks"].append(kf.reshape(Bs, Ts, N_SUB_HEADS, HEAD_DIM))
        outs["vs"].append(vf_s.reshape(Bs, Ts, N_DIFF_HEADS, V_DIM))
        outs["cs"].append(u.reshape(Bs, Ts, D_CONV)[:, Ts - CONV_HIST:])

    ktf, vf = kv_prompt
    new_k_prompt = jnp.transpose(ktf.reshape(depth, B, N_SUB_HEADS, HEAD_DIM, T), (0, 1, 4, 2, 3))
    new_v_prompt = vf.reshape(depth, B, T, N_DIFF_HEADS, V_DIM)
    return (xp.reshape(B, T, D_MODEL), xs.reshape(Bs, Ts, D_MODEL),
            new_k_prompt, new_v_prompt, jnp.stack(outs["cp"]),
            jnp.stack(outs["ks"]), jnp.stack(outs["vs"]), jnp.stack(outs["cs"]))
```

```python
import functools
import math

import jax
import jax.numpy as jnp
from jax import lax
from jax.experimental import pallas as pl
from jax.experimental.pallas import tpu as pltpu

D_MODEL = 1024
D_ATTN = 512
D_CONV = 512
N_DIFF_HEADS = 4
N_SUB_HEADS = 8
HEAD_DIM = 64
HALF = HEAD_DIM // 2
V_DIM = 128
LANES = 128
SUBLANES = 8
CHUNK = 64
CONV_WIDTH = 31
CONV_HIST = CONV_WIDTH - 1
HALO = 32
CONV_ROWS = 64
D_FF = 4096
FF_CHUNK = 1024
ROPE_THETA = 10000.0
LN_EPS = 1e-5
RMS_EPS = 1e-5
DEPTH = 2
DEEPNORM_ALPHA = (2.0 * DEPTH) ** 0.25
NEG_INF = -1e30
LOG2E = math.log2(math.e)
Q_SCALE = HEAD_DIM ** -0.5 * LOG2E
VMEM_LIMIT_BYTES = 56 * 1024 * 1024

BF16 = jnp.bfloat16
F32 = jnp.float32


def _lambda_init(layer):
    return 0.8 - 0.6 * math.exp(-0.3 * layer)


def _params(n_axes):
    return pltpu.CompilerParams(dimension_semantics=("arbitrary",) * n_axes,
                                vmem_limit_bytes=VMEM_LIMIT_BYTES)


def _layer_spec(param, layer):
    rest = param.shape[1:]
    return pl.BlockSpec((None,) + rest, lambda *_: (layer,) + (0,) * len(rest),
                        pipeline_mode=pl.Buffered(1))


def _layer_norm(z, g, b):
    mu = jnp.mean(z, axis=-1, keepdims=True)
    d = z - mu
    var = jnp.mean(d * d, axis=-1, keepdims=True)
    return d * lax.rsqrt(var + LN_EPS) * g + b


def _dot_nt(a, b):
    return lax.dot_general(a, b, (((1,), (1,)), ((), ())), preferred_element_type=F32)


def _conv_chunk(win_ref, base, cw_ref, cb_ref, cg_ref, cbeta_ref):
    tiles = [_conv_lanes(win_ref, base, cw_ref, slice(lt * LANES, (lt + 1) * LANES))
             for lt in range(D_CONV // LANES)]
    return _conv_finish(jnp.concatenate(tiles, axis=1), cb_ref, cg_ref, cbeta_ref)


def _conv_lanes(win_ref, base, cw_ref, lanes):
    off = HALO - CONV_HIST
    x = win_ref[pl.ds(base, CONV_ROWS + HALO), lanes]
    acc = None
    for r in range(SUBLANES):
        rows = CONV_ROWS + SUBLANES if r else CONV_ROWS
        part = None
        for a in range((off + CONV_WIDTH - 1 - r) // SUBLANES + 1):
            j = SUBLANES * a + r - off
            if j < 0:
                continue
            term = cw_ref[j:j + 1, lanes] * x[SUBLANES * a:SUBLANES * a + rows]
            part = term if part is None else part + term
        if r:
            part = pltpu.roll(part, rows - r, 0)[:CONV_ROWS]
        acc = part if acc is None else acc + part
    return acc


def _conv_finish(c, cb_ref, cg_ref, cbeta_ref):
    c = _layer_norm(c + cb_ref[...], cg_ref[...], cbeta_ref[...])
    return c * jax.nn.sigmoid(c)


def _conv_rows_kernel(u_ref, halo_ref, cw_ref, cb_ref, cg_ref, cbeta_ref, c_ref, win_sc, *, seq):
    for b in range(u_ref.shape[0] // seq):
        win = win_sc.at[b]
        win[0:HALO, :] = halo_ref[b * HALO:(b + 1) * HALO, :]
        win[HALO:HALO + seq, :] = u_ref[b * seq:(b + 1) * seq, :]
        for rc in range(seq // CONV_ROWS):
            base = rc * CONV_ROWS
            c_ref[b * seq + base:b * seq + base + CONV_ROWS, :] = _conv_chunk(
                win, base, cw_ref, cb_ref, cg_ref, cbeta_ref).astype(BF16)


def _conv_rows(u, halo, conv_params, layer, seq, tm):
    n = u.shape[0]
    per_tile = tm // seq
    return pl.pallas_call(
        functools.partial(_conv_rows_kernel, seq=seq),
        grid=(n // tm,),
        in_specs=[pl.BlockSpec((tm, D_CONV), lambda i: (i, 0)),
                  pl.BlockSpec((None, per_tile * HALO, D_CONV), lambda i: (layer, i, 0))]
                 + [_layer_spec(p, layer) for p in conv_params],
        out_specs=pl.BlockSpec((tm, D_CONV), lambda i: (i, 0)),
        out_shape=jax.ShapeDtypeStruct((n, D_CONV), BF16),
        scratch_shapes=[pltpu.VMEM((per_tile, HALO + seq, D_CONV), F32)],
        compiler_params=_params(1),
        name="conv_rows",
    )(u, halo, *conv_params)


def _rope_rows(p, cos_ref, sin_ref):
    n = p.shape[1] // LANES
    cos = jnp.concatenate([cos_ref[...]] * n, axis=1)
    sin = jnp.concatenate([sin_ref[...]] * n, axis=1)
    lane = lax.broadcasted_iota(jnp.int32, p.shape, 1)
    partner = jnp.where((lane % HEAD_DIM) < HALF,
                        pltpu.roll(p, p.shape[1] - HALF, 1),
                        pltpu.roll(p, HALF, 1))
    return p * cos + partner * sin


def _rope_cols(pt, cos_ref, sin_ref):
    cos = cos_ref[...]
    sin = sin_ref[...]
    parts = []
    for s in range(pt.shape[0] // HEAD_DIM):
        a = pt[s * HEAD_DIM:s * HEAD_DIM + HALF]
        b = pt[s * HEAD_DIM + HALF:(s + 1) * HEAD_DIM]
        parts += [a * cos - b * sin, b * cos + a * sin]
    return jnp.concatenate(parts, axis=0)


def _proj(xb, w_ref, c0, width):
    return jnp.dot(xb, w_ref[:, c0:c0 + width], preferred_element_type=F32)


def _glu_lanes(xb, w_ref, lt):
    r = _proj(xb, w_ref, 3 * D_ATTN + 2 * LANES * lt, 2 * LANES)
    return r[:, :LANES] * jax.nn.sigmoid(r[:, LANES:])


def _glu(xb, w_ref):
    return jnp.concatenate([_glu_lanes(xb, w_ref, lt) for lt in range(D_CONV // LANES)], axis=1)


def _in_proj_rows_kernel(x_ref, w_ref, cos_ref, sin_ref, q_ref, kf_ref, kb_ref, vf_ref, vb_ref, u_ref):
    xb = x_ref[...].astype(BF16)
    q_ref[...] = (_rope_rows(_proj(xb, w_ref, 0, D_ATTN), cos_ref, sin_ref) * Q_SCALE).astype(BF16)
    k = _rope_rows(_proj(xb, w_ref, D_ATTN, D_ATTN), cos_ref, sin_ref)
    kf_ref[...] = k
    kb_ref[...] = k.astype(BF16)
    v = _proj(xb, w_ref, 2 * D_ATTN, D_ATTN)
    vf_ref[...] = v
    vb_ref[...] = v.astype(BF16)
    u_ref[...] = _glu(xb, w_ref)


def _in_proj_rows(x, w_in, cos, sin, layer, tm):
    n = x.shape[0]
    n_tab = cos.shape[0] // tm
    tok = lambda width: pl.BlockSpec((tm, width), lambda i: (i, 0))
    tab = pl.BlockSpec((tm, LANES), lambda i: (i % n_tab, 0))
    return pl.pallas_call(
        _in_proj_rows_kernel,
        grid=(n // tm,),
        in_specs=[tok(D_MODEL), _layer_spec(w_in, layer), tab, tab],
        out_specs=[tok(D_ATTN)] * 5 + [tok(D_CONV)],
        out_shape=[jax.ShapeDtypeStruct((n, D_ATTN), BF16),
                   jax.ShapeDtypeStruct((n, D_ATTN), F32),
                   jax.ShapeDtypeStruct((n, D_ATTN), BF16),
                   jax.ShapeDtypeStruct((n, D_ATTN), F32),
                   jax.ShapeDtypeStruct((n, D_ATTN), BF16),
                   jax.ShapeDtypeStruct((n, D_CONV), F32)],
        compiler_params=_params(1),
        name="in_proj_rows",
    )(x, w_in, cos, sin)


def _in_proj_conv_kernel(x_ref, w_ref, wkt_ref, cos_ref, sin_ref, cost_ref, sint_ref,
                         cw_ref, cb_ref, cg_ref, cbeta_ref, *rest, tiles_per_seq):
    q_ref, ktf_ref, ktb_ref, vf_ref, vb_ref, c_ref, tail_ref, win_sc, pre_sc = rest[-9:]
    tm = x_ref.shape[0]
    first = (pl.program_id(0) % tiles_per_seq) == 0

    @pl.when(first)
    def _():
        win_sc[0:HALO, :] = jnp.zeros((HALO, D_CONV), F32)

    @pl.when(jnp.logical_not(first))
    def _():
        win_sc[0:HALO, :] = win_sc[tm:tm + HALO, :]

    xb = x_ref[...].astype(BF16)

    def proj_q():
        q_ref[...] = (_rope_rows(_proj(xb, w_ref, 0, D_ATTN), cos_ref, sin_ref) * Q_SCALE).astype(BF16)

    def proj_k():
        kt = _rope_cols(_dot_nt(wkt_ref[...], xb), cost_ref, sint_ref)
        ktf_ref[...] = kt
        ktb_ref[...] = kt.astype(BF16)

    def proj_v():
        v = _proj(xb, w_ref, 2 * D_ATTN, D_ATTN)
        vb_ref[...] = v.astype(BF16)
        for h in range(N_DIFF_HEADS):
            vf_ref[pl.ds(h, tm, stride=N_DIFF_HEADS), :] = v[:, h * V_DIM:(h + 1) * V_DIM]

    projections = {1: (proj_q, proj_k), 2: (proj_v,)}
    for lt in range(D_CONV // LANES):
        lanes = slice(lt * LANES, (lt + 1) * LANES)
        u = _glu_lanes(xb, w_ref, lt)
        win_sc[HALO:HALO + tm, lanes] = u
        tail_ref[:, lanes] = u[tm - HALO:tm]
        for proj in projections.get(lt, ()):
            proj()
        for rc in range(tm // CONV_ROWS):
            base = rc * CONV_ROWS
            pre_sc[base:base + CONV_ROWS, lanes] = _conv_lanes(win_sc, base, cw_ref, lanes)
    for rc in range(tm // CONV_ROWS):
        rows = slice(rc * CONV_ROWS, (rc + 1) * CONV_ROWS)
        c_ref[rows, :] = _conv_finish(pre_sc[rows, :], cb_ref, cg_ref, cbeta_ref).astype(BF16)


def _in_proj_conv(x, w_in, wk_t, cos, sin, cos_t, sin_t, conv_params, layer, batch, seq, tm, prev):
    n = x.shape[0]
    nt = seq // tm
    depth = w_in.shape[0]
    tok = lambda width: pl.BlockSpec((tm, width), lambda i: (i, 0))
    tab = pl.BlockSpec((tm, LANES), lambda i: (i % nt, 0))
    tab_t = pl.BlockSpec((HALF, tm), lambda i: (0, i % nt))
    in_specs = ([tok(D_MODEL), _layer_spec(w_in, layer), _layer_spec(wk_t, layer), tab, tab, tab_t, tab_t]
                + [_layer_spec(p, layer) for p in conv_params])
    args = [x, w_in, wk_t, cos, sin, cos_t, sin_t, *conv_params]
    aliases = {}
    if prev is not None:
        in_specs += [pl.BlockSpec(memory_space=pl.ANY)] * 2
        aliases = {len(args): 1, len(args) + 1: 3}
        args += list(prev)
    out_specs = [tok(D_ATTN),
                 pl.BlockSpec((None, None, D_ATTN, tm), lambda i: (layer, i // nt, 0, i % nt)),
                 pl.BlockSpec((None, D_ATTN, tm), lambda i: (i // nt, 0, i % nt)),
                 pl.BlockSpec((None, tm * N_DIFF_HEADS, V_DIM), lambda i: (layer, i, 0)),
                 tok(D_ATTN), tok(D_CONV),
                 pl.BlockSpec((None, HALO, D_CONV), lambda i: (i // nt, 0, 0))]
    out_shape = [jax.ShapeDtypeStruct((n, D_ATTN), BF16),
                 jax.ShapeDtypeStruct((depth, batch, D_ATTN, seq), F32),
                 jax.ShapeDtypeStruct((batch, D_ATTN, seq), BF16),
                 jax.ShapeDtypeStruct((depth, n * N_DIFF_HEADS, V_DIM), F32),
                 jax.ShapeDtypeStruct((n, D_ATTN), BF16),
                 jax.ShapeDtypeStruct((n, D_CONV), BF16),
                 jax.ShapeDtypeStruct((batch, HALO, D_CONV), F32)]
    return pl.pallas_call(
        functools.partial(_in_proj_conv_kernel, tiles_per_seq=nt),
        grid=(n // tm,),
        in_specs=in_specs,
        out_specs=out_specs,
        out_shape=out_shape,
        input_output_aliases=aliases,
        scratch_shapes=[pltpu.VMEM((HALO + tm, D_CONV), F32), pltpu.VMEM((tm, D_CONV), F32)],
        compiler_params=_params(1),
        name="in_proj_conv",
    )(*args)


def _stack_sub_heads(q):
    lane = lax.broadcasted_iota(jnp.int32, q.shape, 1)
    zero = jnp.zeros_like(q)
    return jnp.concatenate([jnp.where(lane < HEAD_DIM, q, zero),
                            jnp.where(lane >= HEAD_DIM, q, zero)], axis=0)


def _lambda(lq1, lk1, lq2, lk2, lam_init):
    s1 = jnp.sum(lq1[...] * lk1[...], axis=1, keepdims=True)
    s2 = jnp.sum(lq2[...] * lk2[...], axis=1, keepdims=True)
    return jnp.exp(s1) - jnp.exp(s2) + lam_init


def _diff_finish(o_all, lam, g, lam_init):
    t = o_all.shape[0] // 2
    o = o_all[:t] - lam * o_all[t:]
    o = o * lax.rsqrt(jnp.mean(o * o, axis=-1, keepdims=True) + RMS_EPS)
    return o * g * (1.0 - lam_init)


def _row_max(s):
    m = s[:, 0:LANES]
    for c in range(1, s.shape[1] // LANES):
        m = jnp.maximum(m, s[:, c * LANES:(c + 1) * LANES])
    return jnp.broadcast_to(jnp.max(m, axis=1, keepdims=True), m.shape)


def _with_ones(v):
    return jnp.concatenate([v, jnp.ones(v.shape, v.dtype)], axis=1)


def _wide(a, width):
    return jnp.concatenate([a] * (width // LANES), axis=1)


def _attn_prompt_kernel(lq1, lk1, lq2, lk2, g_ref, q_ref, kt_ref, v_ref, o_ref, *scratch,
                        lam_init, tq):
    m_scs = scratch[:N_DIFF_HEADS]
    acc_scs = scratch[N_DIFF_HEADS:]
    qi = pl.program_id(1)
    heads = [slice(h * V_DIM, (h + 1) * V_DIM) for h in range(N_DIFF_HEADS)]
    qqs = [_stack_sub_heads(q_ref[:, cols]) for cols in heads]

    def step(start, width, diagonal):
        if diagonal:
            shape = (2 * tq, tq)
            q_chunk = (lax.broadcasted_iota(jnp.int32, shape, 0) % tq) // CHUNK
            k_chunk = lax.broadcasted_iota(jnp.int32, shape, 1) // CHUNK
            visible = k_chunk <= q_chunk
        scores = [jnp.dot(qq, kt_ref[cols, pl.ds(start, width)], preferred_element_type=F32)
                  for cols, qq in zip(heads, qqs)]
        for cols, s, m_sc, acc_sc in zip(heads, scores, m_scs, acc_scs):
            v = v_ref[pl.ds(start, width), cols]
            if diagonal:
                last = jnp.where(visible, s[:, width - tq:], NEG_INF)
                s = last if width == tq else jnp.concatenate([s[:, :width - tq], last], axis=1)
                m_new = _row_max(s)
            else:
                m_prev = m_sc[...]
                m_new = jnp.maximum(m_prev, _row_max(s))
            p = jnp.exp2(s - _wide(m_new, width)).astype(BF16)
            pv = jnp.dot(p, _with_ones(v), preferred_element_type=F32)
            if diagonal:
                acc_sc[...] = pv
            else:
                acc_sc[...] = _wide(jnp.exp2(m_prev - m_new), 2 * V_DIM) * acc_sc[...] + pv
            m_sc[...] = m_new

    for extra in range(WIDE_BLOCKS):
        @pl.when(qi % WIDE_BLOCKS == extra)
        def _():
            step(pl.multiple_of((qi - extra) * tq, tq), (extra + 1) * tq, True)

    def body(kb, carry):
        step(pl.multiple_of(kb * WIDE_BLOCKS * tq, WIDE_BLOCKS * tq), WIDE_BLOCKS * tq, False)
        return carry

    lax.fori_loop(0, qi // WIDE_BLOCKS, body, 0)

    lam = _lambda(lq1, lk1, lq2, lk2, lam_init)
    for cols, acc_sc in zip(heads, acc_scs):
        acc = acc_sc[...]
        o_all = acc[:, :V_DIM] / acc[:, V_DIM:]
        o_ref[:, cols] = _diff_finish(o_all, lam, g_ref[...], lam_init).astype(o_ref.dtype)


def _attn_prompt(q, kt, v, attn_params, layer, batch, seq, tq):
    nq = seq // tq
    kt_spec = pl.BlockSpec((None, D_ATTN, seq), lambda b, i: (b, 0, 0))
    v_spec = pl.BlockSpec((seq, D_ATTN), lambda b, i: (b, 0))
    q_spec = pl.BlockSpec((tq, D_ATTN), lambda b, i: (b * nq + i, 0))
    return pl.pallas_call(
        functools.partial(_attn_prompt_kernel, lam_init=_lambda_init(layer), tq=tq),
        grid=(batch, nq),
        in_specs=[_layer_spec(p, layer) for p in attn_params] + [q_spec, kt_spec, v_spec],
        out_specs=q_spec,
        out_shape=jax.ShapeDtypeStruct((batch * seq, D_ATTN), BF16),
        scratch_shapes=([pltpu.VMEM((2 * tq, LANES), F32)] * N_DIFF_HEADS
                        + [pltpu.VMEM((2 * tq, 2 * V_DIM), F32)] * N_DIFF_HEADS),
        compiler_params=_params(2),
        name="attn_prompt",
    )(*attn_params, q, kt, v)


def _attn_sample_kernel(lq1, lk1, lq2, lk2, g_ref, q_ref, kn_ref, vn_ref, ckt_ref, cv_ref, o_ref,
                        *, lam_init, past):
    lam = _lambda(lq1, lk1, lq2, lk2, lam_init)
    for h in range(N_DIFF_HEADS):
        cols = slice(h * V_DIM, (h + 1) * V_DIM)
        qq = _stack_sub_heads(q_ref[:, cols])
        s_past = jnp.dot(qq, ckt_ref[cols, :].astype(BF16), preferred_element_type=F32)
        s_new = _dot_nt(qq, kn_ref[:, cols])
        m = jnp.maximum(_row_max(s_past), jnp.max(s_new, axis=1, keepdims=True))
        p_past = jnp.exp2(s_past - _wide(m, past)).astype(BF16)
        p_new = jnp.exp2(s_new - m[:, :s_new.shape[1]]).astype(BF16)
        v_past = cv_ref[pl.ds(h, past, stride=N_DIFF_HEADS), :].astype(BF16)
        acc = (jnp.dot(p_past, _with_ones(v_past), preferred_element_type=F32)
               + jnp.dot(p_new, _with_ones(vn_ref[:, cols]), preferred_element_type=F32))
        o_all = acc[:, :V_DIM] / acc[:, V_DIM:]
        o_ref[:, cols] = _diff_finish(o_all, lam, g_ref[...], lam_init).astype(o_ref.dtype)


def _attn_sample(q, k, v, cache_kt, cache_v, attn_params, layer, batch, seq, past):
    new_spec = pl.BlockSpec((seq, D_ATTN), lambda b: (b, 0))
    return pl.pallas_call(
        functools.partial(_attn_sample_kernel, lam_init=_lambda_init(layer), past=past),
        grid=(batch,),
        in_specs=[_layer_spec(p, layer) for p in attn_params]
                 + [new_spec, new_spec, new_spec,
                    pl.BlockSpec((None, None, D_ATTN, past), lambda b: (layer, b, 0, 0)),
                    pl.BlockSpec((None, None, past * N_DIFF_HEADS, V_DIM), lambda b: (layer, b, 0, 0))],
        out_specs=new_spec,
        out_shape=jax.ShapeDtypeStruct((batch * seq, D_ATTN), BF16),
        compiler_params=_params(1),
        name="attn_sample",
    )(*attn_params, q, k, v, cache_kt, cache_v)


def _post_kernel(attn_ref, c_ref, x_ref, wo_ref, g1_ref, b1_ref, w1_ref, w2_ref, g2_ref, b2_ref, o_ref):
    y = (jnp.dot(attn_ref[...], wo_ref[0:D_ATTN, :], preferred_element_type=F32)
         + jnp.dot(c_ref[...], wo_ref[D_ATTN:D_ATTN + D_CONV, :], preferred_element_type=F32))
    x1 = _layer_norm(DEEPNORM_ALPHA * x_ref[...] + y, g1_ref[...], b1_ref[...])
    xb = x1.astype(BF16)
    acc = jnp.zeros(x1.shape, F32)
    for c in range(D_FF // FF_CHUNK):
        h = jnp.dot(xb, w1_ref[:, c * FF_CHUNK:(c + 1) * FF_CHUNK], preferred_element_type=F32)
        h = jnp.maximum(h, 0.0)
        h = (h * h).astype(BF16)
        acc = acc + jnp.dot(h, w2_ref[c * FF_CHUNK:(c + 1) * FF_CHUNK, :], preferred_element_type=F32)
    o_ref[...] = _layer_norm(DEEPNORM_ALPHA * x1 + acc, g2_ref[...], b2_ref[...])


def _post(attn, c, x, post_params, layer, tm):
    n = x.shape[0]
    tok = lambda width: pl.BlockSpec((tm, width), lambda i: (i, 0))
    return pl.pallas_call(
        _post_kernel,
        grid=(n // tm,),
        in_specs=[tok(D_ATTN), tok(D_CONV), tok(D_MODEL)] + [_layer_spec(p, layer) for p in post_params],
        out_specs=tok(D_MODEL),
        out_shape=jax.ShapeDtypeStruct((n, D_MODEL), F32),
        compiler_params=_params(1),
        name="post",
    )(attn, c, x, *post_params)


def _rope_angles(pos):
    inv = ROPE_THETA ** (-jnp.arange(HALF, dtype=F32) / HALF)
    ang = pos.astype(F32)[:, None] * inv[None, :]
    return jnp.cos(ang), jnp.sin(ang)


def _rope_tables(pos):
    cos, sin = _rope_angles(pos)
    return (jnp.concatenate([cos, cos, cos, cos], axis=1),
            jnp.concatenate([-sin, sin, -sin, sin], axis=1))


def _rows(a):
    return a.reshape(a.shape[0], 1, a.shape[1])


TM = 512
TQ = 256
WIDE_BLOCKS = 4


def kernel(x_prompt, x_sample, cache_k, cache_v, cache_conv, w_in, lambda_q1, lambda_k1, lambda_q2,
           lambda_k2, subln_g, conv_w, conv_b, conv_ln_g, conv_ln_b, w_out, ln1_g, ln1_b, w_ff1,
           w_ff2, ln2_g, ln2_b):
    B, T, _ = x_prompt.shape
    Bs, Ts, _ = x_sample.shape
    P = cache_k.shape[2]
    depth = w_in.shape[0]

    cols = [w_in[:, :, :3 * D_ATTN]]
    for lt in range(D_CONV // LANES):
        for c0 in (3 * D_ATTN, 3 * D_ATTN + D_CONV):
            cols.append(w_in[:, :, c0 + lt * LANES:c0 + (lt + 1) * LANES])
    w_in_b = jnp.concatenate(cols, axis=2).astype(BF16)
    wk_t = jnp.swapaxes(w_in_b[:, :, D_ATTN:2 * D_ATTN], 1, 2)
    w_out_b = w_out.astype(BF16)
    w_ff1_b = w_ff1.astype(BF16)
    w_ff2_b = w_ff2.astype(BF16)

    pos_p = jnp.arange(T)
    cos_p, sin_p = _rope_tables(pos_p)
    cos_pt, sin_pt = (a.T for a in _rope_angles(pos_p))
    cos_s, sin_s = _rope_tables(P + jnp.arange(Ts))
    reps = TM // Ts
    cos_s = jnp.tile(cos_s, (reps, 1))
    sin_s = jnp.tile(sin_s, (reps, 1))

    xp = x_prompt.reshape(B * T, D_MODEL)
    xs = x_sample.reshape(Bs * Ts, D_MODEL)
    cache_kt = jnp.transpose(cache_k, (0, 1, 3, 4, 2)).reshape(depth, Bs, D_ATTN, P)
    cache_vr = cache_v.reshape(depth, Bs, P * N_DIFF_HEADS, V_DIM)
    hist_s = jnp.pad(cache_conv, ((0, 0), (0, 0), (HALO - CONV_HIST, 0), (0, 0)))
    hist_s = hist_s.reshape(depth, Bs * HALO, D_CONV)

    attn_params = tuple(_rows(a) for a in (lambda_q1, lambda_k1, lambda_q2, lambda_k2, subln_g))
    conv_params = (conv_w, _rows(conv_b), _rows(conv_ln_g), _rows(conv_ln_b))
    post_params = (w_out_b, _rows(ln1_g), _rows(ln1_b), w_ff1_b, w_ff2_b, _rows(ln2_g), _rows(ln2_b))

    outs = {k: [] for k in ("cp", "ks", "vs", "cs")}
    kv_prompt = None
    for l in range(depth):
        q, ktf, ktb, vf, vb, c, tail = _in_proj_conv(xp, w_in_b, wk_t, cos_p, sin_p, cos_pt, sin_pt,
                                                     conv_params, l, B, T, TM, kv_prompt)
        kv_prompt = (ktf, vf)
        attn = _attn_prompt(q, ktb, vb, attn_params, l, B, T, TQ)
        xp = _post(attn, c, xp, post_params, l, TM)
        outs["cp"].append(tail[:, HALO - CONV_HIST:])

        q, kf, kb, vf_s, vb, u = _in_proj_rows(xs, w_in_b, cos_s, sin_s, l, TM)
        attn = _attn_sample(q, kb, vb, cache_kt, cache_vr, attn_params, l, Bs, Ts, P)
        c = _conv_rows(u, hist_s, conv_params, l, Ts, TM)
        xs = _post(attn, c, xs, post_params, l, TM)
        outs["ks"].append(kf.reshape(Bs, Ts, N_SUB_HEADS, HEAD_DIM))
        outs["vs"].append(vf_s.reshape(Bs, Ts, N_DIFF_HEADS, V_DIM))
        outs["cs"].append(u.reshape(Bs, Ts, D_CONV)[:, Ts - CONV_HIST:])

    ktf, vf = kv_prompt
    new_k_prompt = jnp.transpose(ktf.reshape(depth, B, N_SUB_HEADS, HEAD_DIM, T), (0, 1, 4, 2, 3))
    new_v_prompt = vf.reshape(depth, B, T, N_DIFF_HEADS, V_DIM)
    return (xp.reshape(B, T, D_MODEL), xs.reshape(Bs, Ts, D_MODEL),
            new_k_prompt, new_v_prompt, jnp.stack(outs["cp"]),
            jnp.stack(outs["ks"]), jnp.stack(outs["vs"]), jnp.stack(outs["cs"]))
```

```python
import functools
import math

import jax
import jax.numpy as jnp
from jax import lax
from jax.experimental import pallas as pl
from jax.experimental.pallas import tpu as pltpu

D_MODEL = 1024
D_ATTN = 512
D_CONV = 512
N_DIFF_HEADS = 4
N_SUB_HEADS = 8
HEAD_DIM = 64
HALF = HEAD_DIM // 2
V_DIM = 128
LANES = 128
SUBLANES = 8
CHUNK = 64
CONV_WIDTH = 31
CONV_HIST = CONV_WIDTH - 1
HALO = 32
CONV_ROWS = 64
D_FF = 4096
FF_CHUNK = 1024
ROPE_THETA = 10000.0
LN_EPS = 1e-5
RMS_EPS = 1e-5
DEPTH = 2
DEEPNORM_ALPHA = (2.0 * DEPTH) ** 0.25
NEG_INF = -1e30
LOG2E = math.log2(math.e)
Q_SCALE = HEAD_DIM ** -0.5 * LOG2E
VMEM_LIMIT_BYTES = 56 * 1024 * 1024

BF16 = jnp.bfloat16
F32 = jnp.float32


def _lambda_init(layer):
    return 0.8 - 0.6 * math.exp(-0.3 * layer)


def _params(n_axes):
    return pltpu.CompilerParams(dimension_semantics=("arbitrary",) * n_axes,
                                vmem_limit_bytes=VMEM_LIMIT_BYTES)


def _layer_spec(param, layer):
    rest = param.shape[1:]
    return pl.BlockSpec((None,) + rest, lambda *_: (layer,) + (0,) * len(rest),
                        pipeline_mode=pl.Buffered(1))


def _layer_norm(z, g, b):
    mu = jnp.mean(z, axis=-1, keepdims=True)
    d = z - mu
    var = jnp.mean(d * d, axis=-1, keepdims=True)
    return d * lax.rsqrt(var + LN_EPS) * g + b


def _dot_nt(a, b):
    return lax.dot_general(a, b, (((1,), (1,)), ((), ())), preferred_element_type=F32)


def _conv_chunk(win_ref, base, cw_ref, cb_ref, cg_ref, cbeta_ref):
    tiles = [_conv_lanes(win_ref, base, cw_ref, slice(lt * LANES, (lt + 1) * LANES))
             for lt in range(D_CONV // LANES)]
    return _conv_finish(jnp.concatenate(tiles, axis=1), cb_ref, cg_ref, cbeta_ref)


def _conv_lanes(win_ref, base, cw_ref, lanes):
    off = HALO - CONV_HIST
    x = win_ref[pl.ds(base, CONV_ROWS + HALO), lanes]
    acc = None
    for r in range(SUBLANES):
        rows = CONV_ROWS + SUBLANES if r else CONV_ROWS
        part = None
        for a in range((off + CONV_WIDTH - 1 - r) // SUBLANES + 1):
            j = SUBLANES * a + r - off
            if j < 0:
                continue
            term = cw_ref[j:j + 1, lanes] * x[SUBLANES * a:SUBLANES * a + rows]
            part = term if part is None else part + term
        if r:
            part = pltpu.roll(part, rows - r, 0)[:CONV_ROWS]
        acc = part if acc is None else acc + part
    return acc


def _conv_finish(c, cb_ref, cg_ref, cbeta_ref):
    c = _layer_norm(c + cb_ref[...], cg_ref[...], cbeta_ref[...])
    return c * jax.nn.sigmoid(c)


def _conv_rows_kernel(u_ref, halo_ref, cw_ref, cb_ref, cg_ref, cbeta_ref, c_ref, win_sc, *, seq):
    for b in range(u_ref.shape[0] // seq):
        win = win_sc.at[b]
        win[0:HALO, :] = halo_ref[b * HALO:(b + 1) * HALO, :]
        win[HALO:HALO + seq, :] = u_ref[b * seq:(b + 1) * seq, :]
        for rc in range(seq // CONV_ROWS):
            base = rc * CONV_ROWS
            c_ref[b * seq + base:b * seq + base + CONV_ROWS, :] = _conv_chunk(
                win, base, cw_ref, cb_ref, cg_ref, cbeta_ref).astype(BF16)


def _conv_rows(u, halo, conv_params, layer, seq, tm):
    n = u.shape[0]
    per_tile = tm // seq
    return pl.pallas_call(
        functools.partial(_conv_rows_kernel, seq=seq),
        grid=(n // tm,),
        in_specs=[pl.BlockSpec((tm, D_CONV), lambda i: (i, 0)),
                  pl.BlockSpec((None, per_tile * HALO, D_CONV), lambda i: (layer, i, 0))]
                 + [_layer_spec(p, layer) for p in conv_params],
        out_specs=pl.BlockSpec((tm, D_CONV), lambda i: (i, 0)),
        out_shape=jax.ShapeDtypeStruct((n, D_CONV), BF16),
        scratch_shapes=[pltpu.VMEM((per_tile, HALO + seq, D_CONV), F32)],
        compiler_params=_params(1),
        name="conv_rows",
    )(u, halo, *conv_params)


def _rope_rows(p, cos_ref, sin_ref):
    n = p.shape[1] // LANES
    cos = jnp.concatenate([cos_ref[...]] * n, axis=1)
    sin = jnp.concatenate([sin_ref[...]] * n, axis=1)
    lane = lax.broadcasted_iota(jnp.int32, p.shape, 1)
    partner = jnp.where((lane % HEAD_DIM) < HALF,
                        pltpu.roll(p, p.shape[1] - HALF, 1),
                        pltpu.roll(p, HALF, 1))
    return p * cos + partner * sin


def _rope_cols(pt, cos_ref, sin_ref):
    cos = cos_ref[...]
    sin = sin_ref[...]
    parts = []
    for s in range(pt.shape[0] // HEAD_DIM):
        a = pt[s * HEAD_DIM:s * HEAD_DIM + HALF]
        b = pt[s * HEAD_DIM + HALF:(s + 1) * HEAD_DIM]
        parts += [a * cos - b * sin, b * cos + a * sin]
    return jnp.concatenate(parts, axis=0)


def _proj(xb, w_ref, c0, width):
    return jnp.dot(xb, w_ref[:, c0:c0 + width], preferred_element_type=F32)


def _glu_lanes(xb, w_ref, lt):
    r = _proj(xb, w_ref, 3 * D_ATTN + 2 * LANES * lt, 2 * LANES)
    return r[:, :LANES] * jax.nn.sigmoid(r[:, LANES:])


def _glu(xb, w_ref):
    return jnp.concatenate([_glu_lanes(xb, w_ref, lt) for lt in range(D_CONV // LANES)], axis=1)


def _in_proj_rows_kernel(x_ref, w_ref, cos_ref, sin_ref, q_ref, kf_ref, kb_ref, vf_ref, vb_ref, u_ref):
    xb = x_ref[...].astype(BF16)
    q_ref[...] = (_rope_rows(_proj(xb, w_ref, 0, D_ATTN), cos_ref, sin_ref) * Q_SCALE).astype(BF16)
    k = _rope_rows(_proj(xb, w_ref, D_ATTN, D_ATTN), cos_ref, sin_ref)
    kf_ref[...] = k
    kb_ref[...] = k.astype(BF16)
    v = _proj(xb, w_ref, 2 * D_ATTN, D_ATTN)
    vf_ref[...] = v
    vb_ref[...] = v.astype(BF16)
    u_ref[...] = _glu(xb, w_ref)


def _in_proj_rows(x, w_in, cos, sin, layer, tm):
    n = x.shape[0]
    n_tab = cos.shape[0] // tm
    tok = lambda width: pl.BlockSpec((tm, width), lambda i: (i, 0))
    tab = pl.BlockSpec((tm, LANES), lambda i: (i % n_tab, 0))
    return pl.pallas_call(
        _in_proj_rows_kernel,
        grid=(n // tm,),
        in_specs=[tok(D_MODEL), _layer_spec(w_in, layer), tab, tab],
        out_specs=[tok(D_ATTN)] * 5 + [tok(D_CONV)],
        out_shape=[jax.ShapeDtypeStruct((n, D_ATTN), BF16),
                   jax.ShapeDtypeStruct((n, D_ATTN), F32),
                   jax.ShapeDtypeStruct((n, D_ATTN), BF16),
                   jax.ShapeDtypeStruct((n, D_ATTN), F32),
                   jax.ShapeDtypeStruct((n, D_ATTN), BF16),
                   jax.ShapeDtypeStruct((n, D_CONV), F32)],
        compiler_params=_params(1),
        name="in_proj_rows",
    )(x, w_in, cos, sin)


def _in_proj_conv_kernel(x_ref, w_ref, wkt_ref, cos_ref, sin_ref, cost_ref, sint_ref,
                         cw_ref, cb_ref, cg_ref, cbeta_ref, *rest, tiles_per_seq):
    q_ref, ktf_ref, ktb_ref, vf_ref, vb_ref, c_ref, tail_ref, win_sc, pre_sc = rest[-9:]
    tm = x_ref.shape[0]
    first = (pl.program_id(0) % tiles_per_seq) == 0

    @pl.when(first)
    def _():
        win_sc[0:HALO, :] = jnp.zeros((HALO, D_CONV), F32)

    @pl.when(jnp.logical_not(first))
    def _():
        win_sc[0:HALO, :] = win_sc[tm:tm + HALO, :]

    xb = x_ref[...].astype(BF16)

    def proj_q():
        q_ref[...] = (_rope_rows(_proj(xb, w_ref, 0, D_ATTN), cos_ref, sin_ref) * Q_SCALE).astype(BF16)

    def proj_k():
        kt = _rope_cols(_dot_nt(wkt_ref[...], xb), cost_ref, sint_ref)
        ktf_ref[...] = kt
        ktb_ref[...] = kt.astype(BF16)

    def proj_v():
        v = _proj(xb, w_ref, 2 * D_ATTN, D_ATTN)
        vb_ref[...] = v.astype(BF16)
        for h in range(N_DIFF_HEADS):
            vf_ref[pl.ds(h, tm, stride=N_DIFF_HEADS), :] = v[:, h * V_DIM:(h + 1) * V_DIM]

    projections = {1: (proj_q, proj_k), 2: (proj_v,)}
    for lt in range(D_CONV // LANES):
        lanes = slice(lt * LANES, (lt + 1) * LANES)
        u = _glu_lanes(xb, w_ref, lt)
        win_sc[HALO:HALO + tm, lanes] = u
        tail_ref[:, lanes] = u[tm - HALO:tm]
        for proj in projections.get(lt, ()):
            proj()
        for rc in range(tm // CONV_ROWS):
            base = rc * CONV_ROWS
            pre_sc[base:base + CONV_ROWS, lanes] = _conv_lanes(win_sc, base, cw_ref, lanes)
    for rc in range(tm // CONV_ROWS):
        rows = slice(rc * CONV_ROWS, (rc + 1) * CONV_ROWS)
        c_ref[rows, :] = _conv_finish(pre_sc[rows, :], cb_ref, cg_ref, cbeta_ref).astype(BF16)


def _in_proj_conv(x, w_in, wk_t, cos, sin, cos_t, sin_t, conv_params, layer, batch, seq, tm, prev):
    n = x.shape[0]
    nt = seq // tm
    depth = w_in.shape[0]
    tok = lambda width: pl.BlockSpec((tm, width), lambda i: (i, 0))
    tab = pl.BlockSpec((tm, LANES), lambda i: (i % nt, 0))
    tab_t = pl.BlockSpec((HALF, tm), lambda i: (0, i % nt))
    in_specs = ([tok(D_MODEL), _layer_spec(w_in, layer), _layer_spec(wk_t, layer), tab, tab, tab_t, tab_t]
                + [_layer_spec(p, layer) for p in conv_params])
    args = [x, w_in, wk_t, cos, sin, cos_t, sin_t, *conv_params]
    aliases = {}
    if prev is not None:
        in_specs += [pl.BlockSpec(memory_space=pl.ANY)] * 2
        aliases = {len(args): 1, len(args) + 1: 3}
        args += list(prev)
    out_specs = [tok(D_ATTN),
                 pl.BlockSpec((None, None, D_ATTN, tm), lambda i: (layer, i // nt, 0, i % nt)),
                 pl.BlockSpec((None, D_ATTN, tm), lambda i: (i // nt, 0, i % nt)),
                 pl.BlockSpec((None, tm * N_DIFF_HEADS, V_DIM), lambda i: (layer, i, 0)),
                 tok(D_ATTN), tok(D_CONV),
                 pl.BlockSpec((None, HALO, D_CONV), lambda i: (i // nt, 0, 0))]
    out_shape = [jax.ShapeDtypeStruct((n, D_ATTN), BF16),
                 jax.ShapeDtypeStruct((depth, batch, D_ATTN, seq), F32),
                 jax.ShapeDtypeStruct((batch, D_ATTN, seq), BF16),
                 jax.ShapeDtypeStruct((depth, n * N_DIFF_HEADS, V_DIM), F32),
                 jax.ShapeDtypeStruct((n, D_ATTN), BF16),
                 jax.ShapeDtypeStruct((n, D_CONV), BF16),
                 jax.ShapeDtypeStruct((batch, HALO, D_CONV), F32)]
    return pl.pallas_call(
        functools.partial(_in_proj_conv_kernel, tiles_per_seq=nt),
        grid=(n // tm,),
        in_specs=in_specs,
        out_specs=out_specs,
        out_shape=out_shape,
        input_output_aliases=aliases,
        scratch_shapes=[pltpu.VMEM((HALO + tm, D_CONV), F32), pltpu.VMEM((tm, D_CONV), F32)],
        compiler_params=_params(1),
        name="in_proj_conv",
    )(*args)


def _stack_sub_heads(q):
    lane = lax.broadcasted_iota(jnp.int32, q.shape, 1)
    zero = jnp.zeros_like(q)
    return jnp.concatenate([jnp.where(lane < HEAD_DIM, q, zero),
                            jnp.where(lane >= HEAD_DIM, q, zero)], axis=0)


def _lambda(lq1, lk1, lq2, lk2, lam_init):
    s1 = jnp.sum(lq1[...] * lk1[...], axis=1, keepdims=True)
    s2 = jnp.sum(lq2[...] * lk2[...], axis=1, keepdims=True)
    return jnp.exp(s1) - jnp.exp(s2) + lam_init


def _diff_finish(o_all, lam, g, lam_init):
    t = o_all.shape[0] // 2
    o = o_all[:t] - lam * o_all[t:]
    o = o * lax.rsqrt(jnp.mean(o * o, axis=-1, keepdims=True) + RMS_EPS)
    return o * g * (1.0 - lam_init)


def _row_max(s):
    m = s[:, 0:LANES]
    for c in range(1, s.shape[1] // LANES):
        m = jnp.maximum(m, s[:, c * LANES:(c + 1) * LANES])
    return jnp.broadcast_to(jnp.max(m, axis=1, keepdims=True), m.shape)


def _with_ones(v):
    return jnp.concatenate([v, jnp.ones(v.shape, v.dtype)], axis=1)


def _wide(a, width):
    return jnp.concatenate([a] * (width // LANES), axis=1)


def _attn_prompt_kernel(lq1, lk1, lq2, lk2, g_ref, q_ref, kt_ref, v_ref, o_ref, *scratch,
                        lam_init, tq):
    m_scs = scratch[:N_DIFF_HEADS]
    acc_scs = scratch[N_DIFF_HEADS:]
    qi = pl.program_id(1)
    heads = [slice(h * V_DIM, (h + 1) * V_DIM) for h in range(N_DIFF_HEADS)]
    qqs = [_stack_sub_heads(q_ref[:, cols]) for cols in heads]

    def step(start, width, diagonal):
        if diagonal:
            shape = (2 * tq, tq)
            q_chunk = (lax.broadcasted_iota(jnp.int32, shape, 0) % tq) // CHUNK
            k_chunk = lax.broadcasted_iota(jnp.int32, shape, 1) // CHUNK
            visible = k_chunk <= q_chunk
        scores = [jnp.dot(qq, kt_ref[cols, pl.ds(start, width)], preferred_element_type=F32)
                  for cols, qq in zip(heads, qqs)]
        for cols, s, m_sc, acc_sc in zip(heads, scores, m_scs, acc_scs):
            v = v_ref[pl.ds(start, width), cols]
            if diagonal:
                last = jnp.where(visible, s[:, width - tq:], NEG_INF)
                s = last if width == tq else jnp.concatenate([s[:, :width - tq], last], axis=1)
                m_new = _row_max(s)
            else:
                m_prev = m_sc[...]
                m_new = jnp.maximum(m_prev, _row_max(s))
            p = jnp.exp2(s - _wide(m_new, width)).astype(BF16)
            pv = jnp.dot(p, _with_ones(v), preferred_element_type=F32)
            if diagonal:
                acc_sc[...] = pv
            else:
                acc_sc[...] = _wide(jnp.exp2(m_prev - m_new), 2 * V_DIM) * acc_sc[...] + pv
            m_sc[...] = m_new

    for extra in range(WIDE_BLOCKS):
        @pl.when(qi % WIDE_BLOCKS == extra)
        def _():
            step(pl.multiple_of((qi - extra) * tq, tq), (extra + 1) * tq, True)

    def body(kb, carry):
        step(pl.multiple_of(kb * WIDE_BLOCKS * tq, WIDE_BLOCKS * tq), WIDE_BLOCKS * tq, False)
        return carry

    lax.fori_loop(0, qi // WIDE_BLOCKS, body, 0)

    lam = _lambda(lq1, lk1, lq2, lk2, lam_init)
    for cols, acc_sc in zip(heads, acc_scs):
        acc = acc_sc[...]
        o_all = acc[:, :V_DIM] / acc[:, V_DIM:]
        o_ref[:, cols] = _diff_finish(o_all, lam, g_ref[...], lam_init).astype(o_ref.dtype)


def _attn_prompt(q, kt, v, attn_params, layer, batch, seq, tq):
    nq = seq // tq
    kt_spec = pl.BlockSpec((None, D_ATTN, seq), lambda b, i: (b, 0, 0))
    v_spec = pl.BlockSpec((seq, D_ATTN), lambda b, i: (b, 0))
    q_spec = pl.BlockSpec((tq, D_ATTN), lambda b, i: (b * nq + i, 0))
    return pl.pallas_call(
        functools.partial(_attn_prompt_kernel, lam_init=_lambda_init(layer), tq=tq),
        grid=(batch, nq),
        in_specs=[_layer_spec(p, layer) for p in attn_params] + [q_spec, kt_spec, v_spec],
        out_specs=q_spec,
        out_shape=jax.ShapeDtypeStruct((batch * seq, D_ATTN), BF16),
        scratch_shapes=([pltpu.VMEM((2 * tq, LANES), F32)] * N_DIFF_HEADS
                        + [pltpu.VMEM((2 * tq, 2 * V_DIM), F32)] * N_DIFF_HEADS),
        compiler_params=_params(2),
        name="attn_prompt",
    )(*attn_params, q, kt, v)


def _attn_sample_kernel(lq1, lk1, lq2, lk2, g_ref, q_ref, kn_ref, vn_ref, ckt_ref, cv_ref, o_ref,
                        *, lam_init, past):
    lam = _lambda(lq1, lk1, lq2, lk2, lam_init)
    for h in range(N_DIFF_HEADS):
        cols = slice(h * V_DIM, (h + 1) * V_DIM)
        qq = _stack_sub_heads(q_ref[:, cols])
        s_past = jnp.dot(qq, ckt_ref[cols, :].astype(BF16), preferred_element_type=F32)
        s_new = _dot_nt(qq, kn_ref[:, cols])
        m = jnp.maximum(_row_max(s_past), jnp.max(s_new, axis=1, keepdims=True))
        p_past = jnp.exp2(s_past - _wide(m, past)).astype(BF16)
        p_new = jnp.exp2(s_new - m[:, :s_new.shape[1]]).astype(BF16)
        v_past = cv_ref[pl.ds(h, past, stride=N_DIFF_HEADS), :].astype(BF16)
        acc = (jnp.dot(p_past, _with_ones(v_past), preferred_element_type=F32)
               + jnp.dot(p_new, _with_ones(vn_ref[:, cols]), preferred_element_type=F32))
        o_all = acc[:, :V_DIM] / acc[:, V_DIM:]
        o_ref[:, cols] = _diff_finish(o_all, lam, g_ref[...], lam_init).astype(o_ref.dtype)


def _attn_sample(q, k, v, cache_kt, cache_v, attn_params, layer, batch, seq, past):
    new_spec = pl.BlockSpec((seq, D_ATTN), lambda b: (b, 0))
    return pl.pallas_call(
        functools.partial(_attn_sample_kernel, lam_init=_lambda_init(layer), past=past),
        grid=(batch,),
        in_specs=[_layer_spec(p, layer) for p in attn_params]
                 + [new_spec, new_spec, new_spec,
                    pl.BlockSpec((None, None, D_ATTN, past), lambda b: (layer, b, 0, 0)),
                    pl.BlockSpec((None, None, past * N_DIFF_HEADS, V_DIM), lambda b: (layer, b, 0, 0))],
        out_specs=new_spec,
        out_shape=jax.ShapeDtypeStruct((batch * seq, D_ATTN), BF16),
        compiler_params=_params(1),
        name="attn_sample",
    )(*attn_params, q, k, v, cache_kt, cache_v)


def _post_kernel(attn_ref, c_ref, x_ref, wo_ref, g1_ref, b1_ref, w1_ref, w2_ref, g2_ref, b2_ref, o_ref):
    tm = x_ref.shape[0]
    halves = [slice(i * (tm // POST_CHAINS), (i + 1) * (tm // POST_CHAINS)) for i in range(POST_CHAINS)]
    n_ff = D_FF // FF_CHUNK

    def out_proj(rows):
        return (jnp.dot(attn_ref[rows, :], wo_ref[0:D_ATTN, :], preferred_element_type=F32)
                + jnp.dot(c_ref[rows, :], wo_ref[D_ATTN:D_ATTN + D_CONV, :], preferred_element_type=F32))

    def ff(c, xb, acc):
        h = jnp.dot(xb, w1_ref[:, c * FF_CHUNK:(c + 1) * FF_CHUNK], preferred_element_type=F32)
        h = jnp.maximum(h, 0.0)
        h = (h * h).astype(BF16)
        return acc + jnp.dot(h, w2_ref[c * FF_CHUNK:(c + 1) * FF_CHUNK, :], preferred_element_type=F32)

    ys = [out_proj(rows) for rows in halves]
    x1s = [_layer_norm(DEEPNORM_ALPHA * x_ref[rows, :] + y, g1_ref[...], b1_ref[...])
           for rows, y in zip(halves, ys)]
    xbs = [x1.astype(BF16) for x1 in x1s]
    accs = [jnp.zeros(x1.shape, F32) for x1 in x1s]
    for c in range(n_ff):
        for i in range(POST_CHAINS):
            accs[i] = ff(c, xbs[i], accs[i])
            if c == n_ff - 1:
                o_ref[halves[i], :] = _layer_norm(DEEPNORM_ALPHA * x1s[i] + accs[i], g2_ref[...], b2_ref[...])


def _post(attn, c, x, post_params, layer, tm):
    n = x.shape[0]
    tok = lambda width: pl.BlockSpec((tm, width), lambda i: (i, 0))
    return pl.pallas_call(
        _post_kernel,
        grid=(n // tm,),
        in_specs=[tok(D_ATTN), tok(D_CONV), tok(D_MODEL)] + [_layer_spec(p, layer) for p in post_params],
        out_specs=tok(D_MODEL),
        out_shape=jax.ShapeDtypeStruct((n, D_MODEL), F32),
        compiler_params=_params(1),
        name="post",
    )(attn, c, x, *post_params)


def _rope_angles(pos):
    inv = ROPE_THETA ** (-jnp.arange(HALF, dtype=F32) / HALF)
    ang = pos.astype(F32)[:, None] * inv[None, :]
    return jnp.cos(ang), jnp.sin(ang)


def _rope_tables(pos):
    cos, sin = _rope_angles(pos)
    return (jnp.concatenate([cos, cos, cos, cos], axis=1),
            jnp.concatenate([-sin, sin, -sin, sin], axis=1))


def _rows(a):
    return a.reshape(a.shape[0], 1, a.shape[1])


TM = 512
TQ = 256
WIDE_BLOCKS = 4
POST_CHAINS = 2


def kernel(x_prompt, x_sample, cache_k, cache_v, cache_conv, w_in, lambda_q1, lambda_k1, lambda_q2,
           lambda_k2, subln_g, conv_w, conv_b, conv_ln_g, conv_ln_b, w_out, ln1_g, ln1_b, w_ff1,
           w_ff2, ln2_g, ln2_b):
    B, T, _ = x_prompt.shape
    Bs, Ts, _ = x_sample.shape
    P = cache_k.shape[2]
    depth = w_in.shape[0]

    cols = [w_in[:, :, :3 * D_ATTN]]
    for lt in range(D_CONV // LANES):
        for c0 in (3 * D_ATTN, 3 * D_ATTN + D_CONV):
            cols.append(w_in[:, :, c0 + lt * LANES:c0 + (lt + 1) * LANES])
    w_in_b = jnp.concatenate(cols, axis=2).astype(BF16)
    wk_t = jnp.swapaxes(w_in_b[:, :, D_ATTN:2 * D_ATTN], 1, 2)
    w_out_b = w_out.astype(BF16)
    w_ff1_b = w_ff1.astype(BF16)
    w_ff2_b = w_ff2.astype(BF16)

    pos_p = jnp.arange(T)
    cos_p, sin_p = _rope_tables(pos_p)
    cos_pt, sin_pt = (a.T for a in _rope_angles(pos_p))
    cos_s, sin_s = _rope_tables(P + jnp.arange(Ts))
    reps = TM // Ts
    cos_s = jnp.tile(cos_s, (reps, 1))
    sin_s = jnp.tile(sin_s, (reps, 1))

    xp = x_prompt.reshape(B * T, D_MODEL)
    xs = x_sample.reshape(Bs * Ts, D_MODEL)
    cache_kt = jnp.transpose(cache_k, (0, 1, 3, 4, 2)).reshape(depth, Bs, D_ATTN, P)
    cache_vr = cache_v.reshape(depth, Bs, P * N_DIFF_HEADS, V_DIM)
    hist_s = jnp.pad(cache_conv, ((0, 0), (0, 0), (HALO - CONV_HIST, 0), (0, 0)))
    hist_s = hist_s.reshape(depth, Bs * HALO, D_CONV)

    attn_params = tuple(_rows(a) for a in (lambda_q1, lambda_k1, lambda_q2, lambda_k2, subln_g))
    conv_params = (conv_w, _rows(conv_b), _rows(conv_ln_g), _rows(conv_ln_b))
    post_params = (w_out_b, _rows(ln1_g), _rows(ln1_b), w_ff1_b, w_ff2_b, _rows(ln2_g), _rows(ln2_b))

    outs = {k: [] for k in ("cp", "ks", "vs", "cs")}
    kv_prompt = None
    for l in range(depth):
        q, ktf, ktb, vf, vb, c, tail = _in_proj_conv(xp, w_in_b, wk_t, cos_p, sin_p, cos_pt, sin_pt,
                                                     conv_params, l, B, T, TM, kv_prompt)
        kv_prompt = (ktf, vf)
        attn = _attn_prompt(q, ktb, vb, attn_params, l, B, T, TQ)
        xp = _post(attn, c, xp, post_params, l, TM)
        outs["cp"].append(tail[:, HALO - CONV_HIST:])

        q, kf, kb, vf_s, vb, u = _in_proj_rows(xs, w_in_b, cos_s, sin_s, l, TM)
        attn = _attn_sample(q, kb, vb, cache_kt, cache_vr, attn_params, l, Bs, Ts, P)
        c = _conv_rows(u, hist_s, conv_params, l, Ts, TM)
        xs = _post(attn, c, xs, post_params, l, TM)
        outs["ks"].append(kf.reshape(Bs, Ts, N_SUB_HEADS, HEAD_DIM))
        outs["vs"].append(vf_s.reshape(Bs, Ts, N_DIFF_HEADS, V_DIM))
        outs["cs"].append(u.reshape(Bs, Ts, D_CONV)[:, Ts - CONV_HIST:])

    ktf, vf = kv_prompt
    new_k_prompt = jnp.transpose(ktf.reshape(depth, B, N_SUB_HEADS, HEAD_DIM, T), (0, 1, 4, 2, 3))
    new_v_prompt = vf.reshape(depth, B, T, N_DIFF_HEADS, V_DIM)
    return (xp.reshape(B, T, D_MODEL), xs.reshape(Bs, Ts, D_MODEL),
            new_k_prompt, new_v_prompt, jnp.stack(outs["cp"]),
            jnp.stack(outs["ks"]), jnp.stack(outs["vs"]), jnp.stack(outs["cs"]))
```

```python
import functools
import math

import jax
import jax.numpy as jnp
from jax import lax
from jax.experimental import pallas as pl
from jax.experimental.pallas import tpu as pltpu

D_MODEL = 1024
D_ATTN = 512
D_CONV = 512
N_DIFF_HEADS = 4
N_SUB_HEADS = 8
HEAD_DIM = 64
HALF = HEAD_DIM // 2
V_DIM = 128
LANES = 128
SUBLANES = 8
CHUNK = 64
CONV_WIDTH = 31
CONV_HIST = CONV_WIDTH - 1
HALO = 32
CONV_ROWS = 64
TILE_CONV_ROWS = 128
D_FF = 4096
FF_CHUNK = 1024
ROPE_THETA = 10000.0
LN_EPS = 1e-5
RMS_EPS = 1e-5
DEPTH = 2
DEEPNORM_ALPHA = (2.0 * DEPTH) ** 0.25
NEG_INF = -1e30
LOG2E = math.log2(math.e)
Q_SCALE = HEAD_DIM ** -0.5 * LOG2E
VMEM_LIMIT_BYTES = 56 * 1024 * 1024

BF16 = jnp.bfloat16
F32 = jnp.float32


def _lambda_init(layer):
    return 0.8 - 0.6 * math.exp(-0.3 * layer)


def _params(n_axes):
    return pltpu.CompilerParams(dimension_semantics=("arbitrary",) * n_axes,
                                vmem_limit_bytes=VMEM_LIMIT_BYTES)


def _layer_spec(param, layer):
    rest = param.shape[1:]
    return pl.BlockSpec((None,) + rest, lambda *_: (layer,) + (0,) * len(rest),
                        pipeline_mode=pl.Buffered(1))


def _layer_norm(z, g, b):
    mu = jnp.mean(z, axis=-1, keepdims=True)
    d = z - mu
    var = jnp.mean(d * d, axis=-1, keepdims=True)
    return d * lax.rsqrt(var + LN_EPS) * g + b


def _dot_nt(a, b):
    return lax.dot_general(a, b, (((1,), (1,)), ((), ())), preferred_element_type=F32)


def _conv_chunk(win_ref, base, cw_ref, cb_ref, cg_ref, cbeta_ref):
    tiles = [_conv_lanes(win_ref, base, cw_ref, slice(lt * LANES, (lt + 1) * LANES))
             for lt in range(D_CONV // LANES)]
    return _conv_finish(jnp.concatenate(tiles, axis=1), cb_ref, cg_ref, cbeta_ref)


def _conv_lanes(win_ref, base, cw_ref, lanes, n_rows=CONV_ROWS):
    off = HALO - CONV_HIST
    x = win_ref[pl.ds(base, n_rows + HALO), lanes]
    acc = None
    for r in range(SUBLANES):
        rows = n_rows + SUBLANES if r else n_rows
        part = None
        for a in range((off + CONV_WIDTH - 1 - r) // SUBLANES + 1):
            j = SUBLANES * a + r - off
            if j < 0:
                continue
            term = cw_ref[j:j + 1, lanes] * x[SUBLANES * a:SUBLANES * a + rows]
            part = term if part is None else part + term
        if r:
            part = pltpu.roll(part, rows - r, 0)[:n_rows]
        acc = part if acc is None else acc + part
    return acc


def _conv_finish(c, cb_ref, cg_ref, cbeta_ref):
    c = _layer_norm(c + cb_ref[...], cg_ref[...], cbeta_ref[...])
    return c * jax.nn.sigmoid(c)


def _conv_rows_kernel(u_ref, halo_ref, cw_ref, cb_ref, cg_ref, cbeta_ref, c_ref, win_sc, *, seq):
    for b in range(u_ref.shape[0] // seq):
        win = win_sc.at[b]
        win[0:HALO, :] = halo_ref[b * HALO:(b + 1) * HALO, :]
        win[HALO:HALO + seq, :] = u_ref[b * seq:(b + 1) * seq, :]
        for rc in range(seq // CONV_ROWS):
            base = rc * CONV_ROWS
            c_ref[b * seq + base:b * seq + base + CONV_ROWS, :] = _conv_chunk(
                win, base, cw_ref, cb_ref, cg_ref, cbeta_ref).astype(BF16)


def _conv_rows(u, halo, conv_params, layer, seq, tm):
    n = u.shape[0]
    per_tile = tm // seq
    return pl.pallas_call(
        functools.partial(_conv_rows_kernel, seq=seq),
        grid=(n // tm,),
        in_specs=[pl.BlockSpec((tm, D_CONV), lambda i: (i, 0)),
                  pl.BlockSpec((None, per_tile * HALO, D_CONV), lambda i: (layer, i, 0))]
                 + [_layer_spec(p, layer) for p in conv_params],
        out_specs=pl.BlockSpec((tm, D_CONV), lambda i: (i, 0)),
        out_shape=jax.ShapeDtypeStruct((n, D_CONV), BF16),
        scratch_shapes=[pltpu.VMEM((per_tile, HALO + seq, D_CONV), F32)],
        compiler_params=_params(1),
        name="conv_rows",
    )(u, halo, *conv_params)


def _rope_rows(p, cos_ref, sin_ref):
    n = p.shape[1] // LANES
    cos = jnp.concatenate([cos_ref[...]] * n, axis=1)
    sin = jnp.concatenate([sin_ref[...]] * n, axis=1)
    lane = lax.broadcasted_iota(jnp.int32, p.shape, 1)
    partner = jnp.where((lane % HEAD_DIM) < HALF,
                        pltpu.roll(p, p.shape[1] - HALF, 1),
                        pltpu.roll(p, HALF, 1))
    return p * cos + partner * sin


def _rope_cols(pt, cos_ref, sin_ref):
    cos = cos_ref[...]
    sin = sin_ref[...]
    parts = []
    for s in range(pt.shape[0] // HEAD_DIM):
        a = pt[s * HEAD_DIM:s * HEAD_DIM + HALF]
        b = pt[s * HEAD_DIM + HALF:(s + 1) * HEAD_DIM]
        parts += [a * cos - b * sin, b * cos + a * sin]
    return jnp.concatenate(parts, axis=0)


def _proj(xb, w_ref, c0, width):
    return jnp.dot(xb, w_ref[:, c0:c0 + width], preferred_element_type=F32)


def _glu_lanes(xb, w_ref, lt):
    r = _proj(xb, w_ref, 3 * D_ATTN + 2 * LANES * lt, 2 * LANES)
    return r[:, :LANES] * jax.nn.sigmoid(r[:, LANES:])


def _glu(xb, w_ref):
    return jnp.concatenate([_glu_lanes(xb, w_ref, lt) for lt in range(D_CONV // LANES)], axis=1)


def _in_proj_rows_kernel(x_ref, w_ref, cos_ref, sin_ref, q_ref, kf_ref, kb_ref, vf_ref, vb_ref, u_ref):
    xb = x_ref[...].astype(BF16)
    q_ref[...] = (_rope_rows(_proj(xb, w_ref, 0, D_ATTN), cos_ref, sin_ref) * Q_SCALE).astype(BF16)
    k = _rope_rows(_proj(xb, w_ref, D_ATTN, D_ATTN), cos_ref, sin_ref)
    kf_ref[...] = k
    kb_ref[...] = k.astype(BF16)
    v = _proj(xb, w_ref, 2 * D_ATTN, D_ATTN)
    vf_ref[...] = v
    vb_ref[...] = v.astype(BF16)
    u_ref[...] = _glu(xb, w_ref)


def _in_proj_rows(x, w_in, cos, sin, layer, tm):
    n = x.shape[0]
    n_tab = cos.shape[0] // tm
    tok = lambda width: pl.BlockSpec((tm, width), lambda i: (i, 0))
    tab = pl.BlockSpec((tm, LANES), lambda i: (i % n_tab, 0))
    return pl.pallas_call(
        _in_proj_rows_kernel,
        grid=(n // tm,),
        in_specs=[tok(D_MODEL), _layer_spec(w_in, layer), tab, tab],
        out_specs=[tok(D_ATTN)] * 5 + [tok(D_CONV)],
        out_shape=[jax.ShapeDtypeStruct((n, D_ATTN), BF16),
                   jax.ShapeDtypeStruct((n, D_ATTN), F32),
                   jax.ShapeDtypeStruct((n, D_ATTN), BF16),
                   jax.ShapeDtypeStruct((n, D_ATTN), F32),
                   jax.ShapeDtypeStruct((n, D_ATTN), BF16),
                   jax.ShapeDtypeStruct((n, D_CONV), F32)],
        compiler_params=_params(1),
        name="in_proj_rows",
    )(x, w_in, cos, sin)


def _in_proj_conv_kernel(x_ref, w_ref, wkt_ref, cos_ref, sin_ref, cost_ref, sint_ref,
                         cw_ref, cb_ref, cg_ref, cbeta_ref, *rest, tiles_per_seq):
    q_ref, ktf_ref, ktb_ref, vf_ref, vb_ref, c_ref, tail_ref, win_sc, pre_sc = rest[-9:]
    tm = x_ref.shape[0]
    first = (pl.program_id(0) % tiles_per_seq) == 0

    @pl.when(first)
    def _():
        win_sc[0:HALO, :] = jnp.zeros((HALO, D_CONV), F32)

    @pl.when(jnp.logical_not(first))
    def _():
        win_sc[0:HALO, :] = win_sc[tm:tm + HALO, :]

    xb = x_ref[...].astype(BF16)

    def proj_q():
        q_ref[...] = (_rope_rows(_proj(xb, w_ref, 0, D_ATTN), cos_ref, sin_ref) * Q_SCALE).astype(BF16)

    def proj_k():
        kt = _rope_cols(_dot_nt(wkt_ref[...], xb), cost_ref, sint_ref)
        ktf_ref[...] = kt
        ktb_ref[...] = kt.astype(BF16)

    def proj_v():
        v = _proj(xb, w_ref, 2 * D_ATTN, D_ATTN)
        vb_ref[...] = v.astype(BF16)
        for h in range(N_DIFF_HEADS):
            vf_ref[pl.ds(h, tm, stride=N_DIFF_HEADS), :] = v[:, h * V_DIM:(h + 1) * V_DIM]

    projections = {1: (proj_q, proj_k), 2: (proj_v,)}
    for lt in range(D_CONV // LANES):
        lanes = slice(lt * LANES, (lt + 1) * LANES)
        u = _glu_lanes(xb, w_ref, lt)
        win_sc[HALO:HALO + tm, lanes] = u
        tail_ref[:, lanes] = u[tm - HALO:tm]
        for proj in projections.get(lt, ()):
            proj()
        for base in range(0, tm, TILE_CONV_ROWS):
            pre_sc[base:base + TILE_CONV_ROWS, lanes] = _conv_lanes(win_sc, base, cw_ref, lanes,
                                                                    TILE_CONV_ROWS)
    for rc in range(tm // CONV_ROWS):
        rows = slice(rc * CONV_ROWS, (rc + 1) * CONV_ROWS)
        c_ref[rows, :] = _conv_finish(pre_sc[rows, :], cb_ref, cg_ref, cbeta_ref).astype(BF16)


def _in_proj_conv(x, w_in, wk_t, cos, sin, cos_t, sin_t, conv_params, layer, batch, seq, tm, prev):
    n = x.shape[0]
    nt = seq // tm
    depth = w_in.shape[0]
    tok = lambda width: pl.BlockSpec((tm, width), lambda i: (i, 0))
    tab = pl.BlockSpec((tm, LANES), lambda i: (i % nt, 0))
    tab_t = pl.BlockSpec((HALF, tm), lambda i: (0, i % nt))
    in_specs = ([tok(D_MODEL), _layer_spec(w_in, layer), _layer_spec(wk_t, layer), tab, tab, tab_t, tab_t]
                + [_layer_spec(p, layer) for p in conv_params])
    args = [x, w_in, wk_t, cos, sin, cos_t, sin_t, *conv_params]
    aliases = {}
    if prev is not None:
        in_specs += [pl.BlockSpec(memory_space=pl.ANY)] * 2
        aliases = {len(args): 1, len(args) + 1: 3}
        args += list(prev)
    out_specs = [tok(D_ATTN),
                 pl.BlockSpec((None, None, D_ATTN, tm), lambda i: (layer, i // nt, 0, i % nt)),
                 pl.BlockSpec((None, D_ATTN, tm), lambda i: (i // nt, 0, i % nt)),
                 pl.BlockSpec((None, tm * N_DIFF_HEADS, V_DIM), lambda i: (layer, i, 0)),
                 tok(D_ATTN), tok(D_CONV),
                 pl.BlockSpec((None, HALO, D_CONV), lambda i: (i // nt, 0, 0))]
    out_shape = [jax.ShapeDtypeStruct((n, D_ATTN), BF16),
                 jax.ShapeDtypeStruct((depth, batch, D_ATTN, seq), F32),
                 jax.ShapeDtypeStruct((batch, D_ATTN, seq), BF16),
                 jax.ShapeDtypeStruct((depth, n * N_DIFF_HEADS, V_DIM), F32),
                 jax.ShapeDtypeStruct((n, D_ATTN), BF16),
                 jax.ShapeDtypeStruct((n, D_CONV), BF16),
                 jax.ShapeDtypeStruct((batch, HALO, D_CONV), F32)]
    return pl.pallas_call(
        functools.partial(_in_proj_conv_kernel, tiles_per_seq=nt),
        grid=(n // tm,),
        in_specs=in_specs,
        out_specs=out_specs,
        out_shape=out_shape,
        input_output_aliases=aliases,
        scratch_shapes=[pltpu.VMEM((HALO + tm, D_CONV), F32), pltpu.VMEM((tm, D_CONV), F32)],
        compiler_params=_params(1),
        name="in_proj_conv",
    )(*args)


def _stack_sub_heads(q):
    lane = lax.broadcasted_iota(jnp.int32, q.shape, 1)
    zero = jnp.zeros_like(q)
    return jnp.concatenate([jnp.where(lane < HEAD_DIM, q, zero),
                            jnp.where(lane >= HEAD_DIM, q, zero)], axis=0)


def _lambda(lq1, lk1, lq2, lk2, lam_init):
    s1 = jnp.sum(lq1[...] * lk1[...], axis=1, keepdims=True)
    s2 = jnp.sum(lq2[...] * lk2[...], axis=1, keepdims=True)
    return jnp.exp(s1) - jnp.exp(s2) + lam_init


def _diff_finish(o_all, lam, g, lam_init):
    t = o_all.shape[0] // 2
    o = o_all[:t] - lam * o_all[t:]
    o = o * lax.rsqrt(jnp.mean(o * o, axis=-1, keepdims=True) + RMS_EPS)
    return o * g * (1.0 - lam_init)


def _row_max(s):
    m = s[:, 0:LANES]
    for c in range(1, s.shape[1] // LANES):
        m = jnp.maximum(m, s[:, c * LANES:(c + 1) * LANES])
    return jnp.broadcast_to(jnp.max(m, axis=1, keepdims=True), m.shape)


def _with_ones(v):
    return jnp.concatenate([v, jnp.ones(v.shape, v.dtype)], axis=1)


def _wide(a, width):
    return jnp.concatenate([a] * (width // LANES), axis=1)


def _attn_prompt_kernel(lq1, lk1, lq2, lk2, g_ref, q_ref, kt_ref, v_ref, o_ref, *scratch,
                        lam_init, tq, n_q):
    m_scs = scratch[:N_DIFF_HEADS]
    acc_scs = scratch[N_DIFF_HEADS:]
    qi = pl.program_id(1)
    heads = [slice(h * V_DIM, (h + 1) * V_DIM) for h in range(N_DIFF_HEADS)]
    qqs = [_stack_sub_heads(q_ref[:, cols]) for cols in heads]

    def step(start, width, diagonal):
        if diagonal:
            shape = (2 * tq, tq)
            q_chunk = (lax.broadcasted_iota(jnp.int32, shape, 0) % tq) // CHUNK
            k_chunk = lax.broadcasted_iota(jnp.int32, shape, 1) // CHUNK
            visible = k_chunk <= q_chunk
        scores = [jnp.dot(qq, kt_ref[cols, pl.ds(start, width)], preferred_element_type=F32)
                  for cols, qq in zip(heads, qqs)]
        for cols, s, m_sc, acc_sc in zip(heads, scores, m_scs, acc_scs):
            v = v_ref[pl.ds(start, width), cols]
            if diagonal:
                last = jnp.where(visible, s[:, width - tq:], NEG_INF)
                s = last if width == tq else jnp.concatenate([s[:, :width - tq], last], axis=1)
                m_new = _row_max(s)
            else:
                m_prev = m_sc[...]
                m_new = jnp.maximum(m_prev, _row_max(s))
            p = jnp.exp2(s - _wide(m_new, width)).astype(BF16)
            pv = jnp.dot(p, _with_ones(v), preferred_element_type=F32)
            if diagonal:
                acc_sc[...] = pv
            else:
                acc_sc[...] = _wide(jnp.exp2(m_prev - m_new), 2 * V_DIM) * acc_sc[...] + pv
            m_sc[...] = m_new

    for extra in range(min(WIDE_BLOCKS, n_q)):
        @pl.when(qi % WIDE_BLOCKS == extra)
        def _():
            step(pl.multiple_of((qi - extra) * tq, tq), (extra + 1) * tq, True)

    if n_q > WIDE_BLOCKS:
        def body(kb, carry):
            step(pl.multiple_of(kb * WIDE_BLOCKS * tq, WIDE_BLOCKS * tq), WIDE_BLOCKS * tq, False)
            return carry

        lax.fori_loop(0, qi // WIDE_BLOCKS, body, 0)

    lam = _lambda(lq1, lk1, lq2, lk2, lam_init)
    for cols, acc_sc in zip(heads, acc_scs):
        acc = acc_sc[...]
        o_all = acc[:, :V_DIM] / acc[:, V_DIM:]
        o_ref[:, cols] = _diff_finish(o_all, lam, g_ref[...], lam_init).astype(o_ref.dtype)


def _attn_prompt(q, kt, v, attn_params, layer, batch, seq, tq):
    nq = seq // tq
    kt_spec = pl.BlockSpec((None, D_ATTN, seq), lambda b, i: (b, 0, 0))
    v_spec = pl.BlockSpec((seq, D_ATTN), lambda b, i: (b, 0))
    q_spec = pl.BlockSpec((tq, D_ATTN), lambda b, i: (b * nq + i, 0))
    return pl.pallas_call(
        functools.partial(_attn_prompt_kernel, lam_init=_lambda_init(layer), tq=tq, n_q=nq),
        grid=(batch, nq),
        in_specs=[_layer_spec(p, layer) for p in attn_params] + [q_spec, kt_spec, v_spec],
        out_specs=q_spec,
        out_shape=jax.ShapeDtypeStruct((batch * seq, D_ATTN), BF16),
        scratch_shapes=([pltpu.VMEM((2 * tq, LANES), F32)] * N_DIFF_HEADS
                        + [pltpu.VMEM((2 * tq, 2 * V_DIM), F32)] * N_DIFF_HEADS),
        compiler_params=_params(2),
        name="attn_prompt",
    )(*attn_params, q, kt, v)


def _attn_sample_kernel(lq1, lk1, lq2, lk2, g_ref, q_ref, kn_ref, vn_ref, ckt_ref, cv_ref, o_ref,
                        *, lam_init, past):
    lam = _lambda(lq1, lk1, lq2, lk2, lam_init)
    for h in range(N_DIFF_HEADS):
        cols = slice(h * V_DIM, (h + 1) * V_DIM)
        qq = _stack_sub_heads(q_ref[:, cols])
        s_past = jnp.dot(qq, ckt_ref[cols, :].astype(BF16), preferred_element_type=F32)
        s_new = _dot_nt(qq, kn_ref[:, cols])
        m = jnp.maximum(_row_max(s_past), jnp.max(s_new, axis=1, keepdims=True))
        p_past = jnp.exp2(s_past - _wide(m, past)).astype(BF16)
        p_new = jnp.exp2(s_new - m[:, :s_new.shape[1]]).astype(BF16)
        v_past = cv_ref[pl.ds(h, past, stride=N_DIFF_HEADS), :].astype(BF16)
        acc = (jnp.dot(p_past, _with_ones(v_past), preferred_element_type=F32)
               + jnp.dot(p_new, _with_ones(vn_ref[:, cols]), preferred_element_type=F32))
        o_all = acc[:, :V_DIM] / acc[:, V_DIM:]
        o_ref[:, cols] = _diff_finish(o_all, lam, g_ref[...], lam_init).astype(o_ref.dtype)


def _attn_sample(q, k, v, cache_kt, cache_v, attn_params, layer, batch, seq, past):
    new_spec = pl.BlockSpec((seq, D_ATTN), lambda b: (b, 0))
    return pl.pallas_call(
        functools.partial(_attn_sample_kernel, lam_init=_lambda_init(layer), past=past),
        grid=(batch,),
        in_specs=[_layer_spec(p, layer) for p in attn_params]
                 + [new_spec, new_spec, new_spec,
                    pl.BlockSpec((None, None, D_ATTN, past), lambda b: (layer, b, 0, 0)),
                    pl.BlockSpec((None, None, past * N_DIFF_HEADS, V_DIM), lambda b: (layer, b, 0, 0))],
        out_specs=new_spec,
        out_shape=jax.ShapeDtypeStruct((batch * seq, D_ATTN), BF16),
        compiler_params=_params(1),
        name="attn_sample",
    )(*attn_params, q, k, v, cache_kt, cache_v)


def _post_kernel(attn_ref, c_ref, x_ref, wo_ref, g1_ref, b1_ref, w1_ref, w2_ref, g2_ref, b2_ref, o_ref):
    tm = x_ref.shape[0]
    halves = [slice(i * (tm // POST_CHAINS), (i + 1) * (tm // POST_CHAINS)) for i in range(POST_CHAINS)]
    n_ff = D_FF // FF_CHUNK

    def out_proj(rows):
        return (jnp.dot(attn_ref[rows, :], wo_ref[0:D_ATTN, :], preferred_element_type=F32)
                + jnp.dot(c_ref[rows, :], wo_ref[D_ATTN:D_ATTN + D_CONV, :], preferred_element_type=F32))

    def ff(c, xb, acc):
        h = jnp.dot(xb, w1_ref[:, c * FF_CHUNK:(c + 1) * FF_CHUNK], preferred_element_type=F32)
        h = jnp.maximum(h, 0.0)
        h = (h * h).astype(BF16)
        return acc + jnp.dot(h, w2_ref[c * FF_CHUNK:(c + 1) * FF_CHUNK, :], preferred_element_type=F32)

    ys = [out_proj(rows) for rows in halves]
    x1s = [_layer_norm(DEEPNORM_ALPHA * x_ref[rows, :] + y, g1_ref[...], b1_ref[...])
           for rows, y in zip(halves, ys)]
    xbs = [x1.astype(BF16) for x1 in x1s]
    accs = [jnp.zeros(x1.shape, F32) for x1 in x1s]
    for c in range(n_ff):
        for i in range(POST_CHAINS):
            accs[i] = ff(c, xbs[i], accs[i])
            if c == n_ff - 1:
                o_ref[halves[i], :] = _layer_norm(DEEPNORM_ALPHA * x1s[i] + accs[i], g2_ref[...], b2_ref[...])


def _post(attn, c, x, post_params, layer, tm):
    n = x.shape[0]
    tok = lambda width: pl.BlockSpec((tm, width), lambda i: (i, 0))
    return pl.pallas_call(
        _post_kernel,
        grid=(n // tm,),
        in_specs=[tok(D_ATTN), tok(D_CONV), tok(D_MODEL)] + [_layer_spec(p, layer) for p in post_params],
        out_specs=tok(D_MODEL),
        out_shape=jax.ShapeDtypeStruct((n, D_MODEL), F32),
        compiler_params=_params(1),
        name="post",
    )(attn, c, x, *post_params)


def _rope_angles(pos):
    inv = ROPE_THETA ** (-jnp.arange(HALF, dtype=F32) / HALF)
    ang = pos.astype(F32)[:, None] * inv[None, :]
    return jnp.cos(ang), jnp.sin(ang)


def _rope_tables(pos):
    cos, sin = _rope_angles(pos)
    return (jnp.concatenate([cos, cos, cos, cos], axis=1),
            jnp.concatenate([-sin, sin, -sin, sin], axis=1))


def _rows(a):
    return a.reshape(a.shape[0], 1, a.shape[1])


TM = 512
TQ = 256
WIDE_BLOCKS = 8
POST_CHAINS = 2


def kernel(x_prompt, x_sample, cache_k, cache_v, cache_conv, w_in, lambda_q1, lambda_k1, lambda_q2,
           lambda_k2, subln_g, conv_w, conv_b, conv_ln_g, conv_ln_b, w_out, ln1_g, ln1_b, w_ff1,
           w_ff2, ln2_g, ln2_b):
    B, T, _ = x_prompt.shape
    Bs, Ts, _ = x_sample.shape
    P = cache_k.shape[2]
    depth = w_in.shape[0]

    cols = [w_in[:, :, :3 * D_ATTN]]
    for lt in range(D_CONV // LANES):
        for c0 in (3 * D_ATTN, 3 * D_ATTN + D_CONV):
            cols.append(w_in[:, :, c0 + lt * LANES:c0 + (lt + 1) * LANES])
    w_in_b = jnp.concatenate(cols, axis=2).astype(BF16)
    wk_t = jnp.swapaxes(w_in_b[:, :, D_ATTN:2 * D_ATTN], 1, 2)
    w_out_b = w_out.astype(BF16)
    w_ff1_b = w_ff1.astype(BF16)
    w_ff2_b = w_ff2.astype(BF16)

    pos_p = jnp.arange(T)
    cos_p, sin_p = _rope_tables(pos_p)
    cos_pt, sin_pt = (a.T for a in _rope_angles(pos_p))
    cos_s, sin_s = _rope_tables(P + jnp.arange(Ts))
    reps = TM // Ts
    cos_s = jnp.tile(cos_s, (reps, 1))
    sin_s = jnp.tile(sin_s, (reps, 1))

    xp = x_prompt.reshape(B * T, D_MODEL)
    xs = x_sample.reshape(Bs * Ts, D_MODEL)
    cache_kt = jnp.transpose(cache_k, (0, 1, 3, 4, 2)).reshape(depth, Bs, D_ATTN, P)
    cache_vr = cache_v.reshape(depth, Bs, P * N_DIFF_HEADS, V_DIM)
    hist_s = jnp.pad(cache_conv, ((0, 0), (0, 0), (HALO - CONV_HIST, 0), (0, 0)))
    hist_s = hist_s.reshape(depth, Bs * HALO, D_CONV)

    attn_params = tuple(_rows(a) for a in (lambda_q1, lambda_k1, lambda_q2, lambda_k2, subln_g))
    conv_params = (conv_w, _rows(conv_b), _rows(conv_ln_g), _rows(conv_ln_b))
    post_params = (w_out_b, _rows(ln1_g), _rows(ln1_b), w_ff1_b, w_ff2_b, _rows(ln2_g), _rows(ln2_b))

    outs = {k: [] for k in ("cp", "ks", "vs", "cs")}
    kv_prompt = None
    for l in range(depth):
        q, ktf, ktb, vf, vb, c, tail = _in_proj_conv(xp, w_in_b, wk_t, cos_p, sin_p, cos_pt, sin_pt,
                                                     conv_params, l, B, T, TM, kv_prompt)
        kv_prompt = (ktf, vf)
        attn = _attn_prompt(q, ktb, vb, attn_params, l, B, T, TQ)
        xp = _post(attn, c, xp, post_params, l, TM)
        outs["cp"].append(tail[:, HALO - CONV_HIST:])

        q, kf, kb, vf_s, vb, u = _in_proj_rows(xs, w_in_b, cos_s, sin_s, l, TM)
        attn = _attn_sample(q, kb, vb, cache_kt, cache_vr, attn_params, l, Bs, Ts, P)
        c = _conv_rows(u, hist_s, conv_params, l, Ts, TM)
        xs = _post(attn, c, xs, post_params, l, TM)
        outs["ks"].append(kf.reshape(Bs, Ts, N_SUB_HEADS, HEAD_DIM))
        outs["vs"].append(vf_s.reshape(Bs, Ts, N_DIFF_HEADS, V_DIM))
        outs["cs"].append(u.reshape(Bs, Ts, D_CONV)[:, Ts - CONV_HIST:])

    ktf, vf = kv_prompt
    new_k_prompt = jnp.transpose(ktf.reshape(depth, B, N_SUB_HEADS, HEAD_DIM, T), (0, 1, 4, 2, 3))
    new_v_prompt = vf.reshape(depth, B, T, N_DIFF_HEADS, V_DIM)
    return (xp.reshape(B, T, D_MODEL), xs.reshape(Bs, Ts, D_MODEL),
            new_k_prompt, new_v_prompt, jnp.stack(outs["cp"]),
            jnp.stack(outs["ks"]), jnp.stack(outs["vs"]), jnp.stack(outs["cs"]))
```

```python
import functools
import math

import jax
import jax.numpy as jnp
from jax import lax
from jax.experimental import pallas as pl
from jax.experimental.pallas import tpu as pltpu

D_MODEL = 1024
D_ATTN = 512
D_CONV = 512
N_DIFF_HEADS = 4
N_SUB_HEADS = 8
HEAD_DIM = 64
HALF = HEAD_DIM // 2
V_DIM = 128
LANES = 128
SUBLANES = 8
CHUNK = 64
CONV_WIDTH = 31
CONV_HIST = CONV_WIDTH - 1
HALO = 32
CONV_ROWS = 64
TILE_CONV_ROWS = 128
D_FF = 4096
FF_CHUNK = 1024
ROPE_THETA = 10000.0
LN_EPS = 1e-5
RMS_EPS = 1e-5
DEPTH = 2
DEEPNORM_ALPHA = (2.0 * DEPTH) ** 0.25
NEG_INF = -1e30
LOG2E = math.log2(math.e)
Q_SCALE = HEAD_DIM ** -0.5 * LOG2E
VMEM_LIMIT_BYTES = 56 * 1024 * 1024

BF16 = jnp.bfloat16
F32 = jnp.float32


def _lambda_init(layer):
    return 0.8 - 0.6 * math.exp(-0.3 * layer)


def _params(n_axes):
    return pltpu.CompilerParams(dimension_semantics=("arbitrary",) * n_axes,
                                vmem_limit_bytes=VMEM_LIMIT_BYTES)


def _layer_spec(param, layer):
    rest = param.shape[1:]
    return pl.BlockSpec((None,) + rest, lambda *_: (layer,) + (0,) * len(rest),
                        pipeline_mode=pl.Buffered(1))


def _layer_norm(z, g, b):
    mu = jnp.mean(z, axis=-1, keepdims=True)
    d = z - mu
    var = jnp.mean(d * d, axis=-1, keepdims=True)
    return d * lax.rsqrt(var + LN_EPS) * g + b


def _dot_nt(a, b):
    return lax.dot_general(a, b, (((1,), (1,)), ((), ())), preferred_element_type=F32)


def _conv_chunk(win_ref, base, cw_ref, cb_ref, cg_ref, cbeta_ref):
    tiles = [_conv_lanes(win_ref, base, cw_ref, slice(lt * LANES, (lt + 1) * LANES))
             for lt in range(D_CONV // LANES)]
    return _conv_finish(jnp.concatenate(tiles, axis=1), cb_ref, cg_ref, cbeta_ref)


def _conv_lanes(win_ref, base, cw_ref, lanes, n_rows=CONV_ROWS):
    off = HALO - CONV_HIST
    x = win_ref[pl.ds(base, n_rows + HALO), lanes]
    acc = None
    for r in range(SUBLANES):
        rows = n_rows + SUBLANES if r else n_rows
        part = None
        for a in range((off + CONV_WIDTH - 1 - r) // SUBLANES + 1):
            j = SUBLANES * a + r - off
            if j < 0:
                continue
            term = cw_ref[j:j + 1, lanes] * x[SUBLANES * a:SUBLANES * a + rows]
            part = term if part is None else part + term
        if r:
            part = pltpu.roll(part, rows - r, 0)[:n_rows]
        acc = part if acc is None else acc + part
    return acc


def _conv_finish(c, cb_ref, cg_ref, cbeta_ref):
    c = _layer_norm(c + cb_ref[...], cg_ref[...], cbeta_ref[...])
    return c * jax.nn.sigmoid(c)


def _conv_rows_kernel(u_ref, halo_ref, cw_ref, cb_ref, cg_ref, cbeta_ref, c_ref, win_sc, *, seq):
    for b in range(u_ref.shape[0] // seq):
        win = win_sc.at[b]
        win[0:HALO, :] = halo_ref[b * HALO:(b + 1) * HALO, :]
        win[HALO:HALO + seq, :] = u_ref[b * seq:(b + 1) * seq, :]
        for rc in range(seq // CONV_ROWS):
            base = rc * CONV_ROWS
            c_ref[b * seq + base:b * seq + base + CONV_ROWS, :] = _conv_chunk(
                win, base, cw_ref, cb_ref, cg_ref, cbeta_ref).astype(BF16)


def _conv_rows(u, halo, conv_params, layer, seq, tm):
    n = u.shape[0]
    per_tile = tm // seq
    return pl.pallas_call(
        functools.partial(_conv_rows_kernel, seq=seq),
        grid=(n // tm,),
        in_specs=[pl.BlockSpec((tm, D_CONV), lambda i: (i, 0)),
                  pl.BlockSpec((None, per_tile * HALO, D_CONV), lambda i: (layer, i, 0))]
                 + [_layer_spec(p, layer) for p in conv_params],
        out_specs=pl.BlockSpec((tm, D_CONV), lambda i: (i, 0)),
        out_shape=jax.ShapeDtypeStruct((n, D_CONV), BF16),
        scratch_shapes=[pltpu.VMEM((per_tile, HALO + seq, D_CONV), F32)],
        compiler_params=_params(1),
        name="conv_rows",
    )(u, halo, *conv_params)


def _rope_rows(p, cos_ref, sin_ref):
    n = p.shape[1] // LANES
    cos = jnp.concatenate([cos_ref[...]] * n, axis=1)
    sin = jnp.concatenate([sin_ref[...]] * n, axis=1)
    lane = lax.broadcasted_iota(jnp.int32, p.shape, 1)
    partner = jnp.where((lane % HEAD_DIM) < HALF,
                        pltpu.roll(p, p.shape[1] - HALF, 1),
                        pltpu.roll(p, HALF, 1))
    return p * cos + partner * sin


def _rope_cols(pt, cos_ref, sin_ref):
    cos = cos_ref[...]
    sin = sin_ref[...]
    parts = []
    for s in range(pt.shape[0] // HEAD_DIM):
        a = pt[s * HEAD_DIM:s * HEAD_DIM + HALF]
        b = pt[s * HEAD_DIM + HALF:(s + 1) * HEAD_DIM]
        parts += [a * cos - b * sin, b * cos + a * sin]
    return jnp.concatenate(parts, axis=0)


def _proj(xb, w_ref, c0, width):
    return jnp.dot(xb, w_ref[:, c0:c0 + width], preferred_element_type=F32)


def _glu_lanes(xb, w_ref, lt):
    r = _proj(xb, w_ref, 3 * D_ATTN + 2 * LANES * lt, 2 * LANES)
    return r[:, :LANES] * jax.nn.sigmoid(r[:, LANES:])


def _glu(xb, w_ref):
    return jnp.concatenate([_glu_lanes(xb, w_ref, lt) for lt in range(D_CONV // LANES)], axis=1)


def _in_proj_rows_kernel(x_ref, w_ref, cos_ref, sin_ref, q_ref, kf_ref, kb_ref, vf_ref, vb_ref, u_ref):
    xb = x_ref[...].astype(BF16)
    q_ref[...] = (_rope_rows(_proj(xb, w_ref, 0, D_ATTN), cos_ref, sin_ref) * Q_SCALE).astype(BF16)
    k = _rope_rows(_proj(xb, w_ref, D_ATTN, D_ATTN), cos_ref, sin_ref)
    kf_ref[...] = k
    kb_ref[...] = k.astype(BF16)
    v = _proj(xb, w_ref, 2 * D_ATTN, D_ATTN)
    vf_ref[...] = v
    vb_ref[...] = v.astype(BF16)
    u_ref[...] = _glu(xb, w_ref)


def _in_proj_rows(x, w_in, cos, sin, layer, tm):
    n = x.shape[0]
    n_tab = cos.shape[0] // tm
    tok = lambda width: pl.BlockSpec((tm, width), lambda i: (i, 0))
    tab = pl.BlockSpec((tm, LANES), lambda i: (i % n_tab, 0))
    return pl.pallas_call(
        _in_proj_rows_kernel,
        grid=(n // tm,),
        in_specs=[tok(D_MODEL), _layer_spec(w_in, layer), tab, tab],
        out_specs=[tok(D_ATTN)] * 5 + [tok(D_CONV)],
        out_shape=[jax.ShapeDtypeStruct((n, D_ATTN), BF16),
                   jax.ShapeDtypeStruct((n, D_ATTN), F32),
                   jax.ShapeDtypeStruct((n, D_ATTN), BF16),
                   jax.ShapeDtypeStruct((n, D_ATTN), F32),
                   jax.ShapeDtypeStruct((n, D_ATTN), BF16),
                   jax.ShapeDtypeStruct((n, D_CONV), F32)],
        compiler_params=_params(1),
        name="in_proj_rows",
    )(x, w_in, cos, sin)


def _in_proj_conv_kernel(x_ref, w_ref, wkt_ref, cos_ref, sin_ref, cost_ref, sint_ref,
                         cw_ref, cb_ref, cg_ref, cbeta_ref, *rest, tiles_per_seq):
    q_ref, ktf_ref, ktb_ref, vf_ref, vb_ref, c_ref, tail_ref, win_sc, pre_sc = rest[-9:]
    tm = x_ref.shape[0]
    first = (pl.program_id(0) % tiles_per_seq) == 0

    @pl.when(first)
    def _():
        win_sc[0:HALO, :] = jnp.zeros((HALO, D_CONV), F32)

    @pl.when(jnp.logical_not(first))
    def _():
        win_sc[0:HALO, :] = win_sc[tm:tm + HALO, :]

    xb = x_ref[...].astype(BF16)

    def proj_q():
        q_ref[...] = (_rope_rows(_proj(xb, w_ref, 0, D_ATTN), cos_ref, sin_ref) * Q_SCALE).astype(BF16)

    def proj_k():
        kt = _rope_cols(_dot_nt(wkt_ref[...], xb), cost_ref, sint_ref)
        ktf_ref[...] = kt
        ktb_ref[...] = kt.astype(BF16)

    def proj_v():
        v = _proj(xb, w_ref, 2 * D_ATTN, D_ATTN)
        vb_ref[...] = v.astype(BF16)
        for h in range(N_DIFF_HEADS):
            vf_ref[pl.ds(h, tm, stride=N_DIFF_HEADS), :] = v[:, h * V_DIM:(h + 1) * V_DIM]

    projections = {1: (proj_q, proj_k), 2: (proj_v,)}
    for lt in range(D_CONV // LANES):
        lanes = slice(lt * LANES, (lt + 1) * LANES)
        u = _glu_lanes(xb, w_ref, lt)
        win_sc[HALO:HALO + tm, lanes] = u
        tail_ref[:, lanes] = u[tm - HALO:tm]
        for proj in projections.get(lt, ()):
            proj()
        for base in range(0, tm, TILE_CONV_ROWS):
            pre_sc[base:base + TILE_CONV_ROWS, lanes] = _conv_lanes(win_sc, base, cw_ref, lanes,
                                                                    TILE_CONV_ROWS)
    for rc in range(tm // CONV_ROWS):
        rows = slice(rc * CONV_ROWS, (rc + 1) * CONV_ROWS)
        c_ref[rows, :] = _conv_finish(pre_sc[rows, :], cb_ref, cg_ref, cbeta_ref).astype(BF16)


def _in_proj_conv(x, w_in, wk_t, cos, sin, cos_t, sin_t, conv_params, layer, batch, seq, tm, prev):
    n = x.shape[0]
    nt = seq // tm
    depth = w_in.shape[0]
    tok = lambda width: pl.BlockSpec((tm, width), lambda i: (i, 0))
    tab = pl.BlockSpec((tm, LANES), lambda i: (i % nt, 0))
    tab_t = pl.BlockSpec((HALF, tm), lambda i: (0, i % nt))
    in_specs = ([tok(D_MODEL), _layer_spec(w_in, layer), _layer_spec(wk_t, layer), tab, tab, tab_t, tab_t]
                + [_layer_spec(p, layer) for p in conv_params])
    args = [x, w_in, wk_t, cos, sin, cos_t, sin_t, *conv_params]
    aliases = {}
    if prev is not None:
        in_specs += [pl.BlockSpec(memory_space=pl.ANY)] * 2
        aliases = {len(args): 1, len(args) + 1: 3}
        args += list(prev)
    out_specs = [tok(D_ATTN),
                 pl.BlockSpec((None, None, D_ATTN, tm), lambda i: (layer, i // nt, 0, i % nt)),
                 pl.BlockSpec((None, D_ATTN, tm), lambda i: (i // nt, 0, i % nt)),
                 pl.BlockSpec((None, tm * N_DIFF_HEADS, V_DIM), lambda i: (layer, i, 0)),
                 tok(D_ATTN), tok(D_CONV),
                 pl.BlockSpec((None, HALO, D_CONV), lambda i: (i // nt, 0, 0))]
    out_shape = [jax.ShapeDtypeStruct((n, D_ATTN), BF16),
                 jax.ShapeDtypeStruct((depth, batch, D_ATTN, seq), F32),
                 jax.ShapeDtypeStruct((batch, D_ATTN, seq), BF16),
                 jax.ShapeDtypeStruct((depth, n * N_DIFF_HEADS, V_DIM), F32),
                 jax.ShapeDtypeStruct((n, D_ATTN), BF16),
                 jax.ShapeDtypeStruct((n, D_CONV), BF16),
                 jax.ShapeDtypeStruct((batch, HALO, D_CONV), F32)]
    return pl.pallas_call(
        functools.partial(_in_proj_conv_kernel, tiles_per_seq=nt),
        grid=(n // tm,),
        in_specs=in_specs,
        out_specs=out_specs,
        out_shape=out_shape,
        input_output_aliases=aliases,
        scratch_shapes=[pltpu.VMEM((HALO + tm, D_CONV), F32), pltpu.VMEM((tm, D_CONV), F32)],
        compiler_params=_params(1),
        name="in_proj_conv",
    )(*args)


def _stack_sub_heads(q):
    lane = lax.broadcasted_iota(jnp.int32, q.shape, 1)
    zero = jnp.zeros_like(q)
    return jnp.concatenate([jnp.where(lane < HEAD_DIM, q, zero),
                            jnp.where(lane >= HEAD_DIM, q, zero)], axis=0)


def _lambda(lq1, lk1, lq2, lk2, lam_init):
    s1 = jnp.sum(lq1[...] * lk1[...], axis=1, keepdims=True)
    s2 = jnp.sum(lq2[...] * lk2[...], axis=1, keepdims=True)
    return jnp.exp(s1) - jnp.exp(s2) + lam_init


def _diff_finish(o_all, lam, g, lam_init):
    t = o_all.shape[0] // 2
    o = o_all[:t] - lam * o_all[t:]
    o = o * lax.rsqrt(jnp.mean(o * o, axis=-1, keepdims=True) + RMS_EPS)
    return o * g * (1.0 - lam_init)


def _row_max(s):
    m = s[:, 0:LANES]
    for c in range(1, s.shape[1] // LANES):
        m = jnp.maximum(m, s[:, c * LANES:(c + 1) * LANES])
    return jnp.broadcast_to(jnp.max(m, axis=1, keepdims=True), m.shape)


def _with_ones(v):
    return jnp.concatenate([v, jnp.ones(v.shape, v.dtype)], axis=1)


def _wide(a, width):
    return jnp.concatenate([a] * (width // LANES), axis=1)


def _attn_prompt_kernel(lq1, lk1, lq2, lk2, g_ref, q_ref, kt_ref, v_ref, o_ref, *scratch,
                        lam_init, tq, n_q):
    m_scs = scratch[:N_DIFF_HEADS]
    acc_scs = scratch[N_DIFF_HEADS:]
    qi = pl.program_id(1)
    heads = [slice(h * V_DIM, (h + 1) * V_DIM) for h in range(N_DIFF_HEADS)]
    qqs = [_stack_sub_heads(q_ref[:, cols]) for cols in heads]
    lam = _lambda(lq1, lk1, lq2, lk2, lam_init)
    single_step = n_q <= WIDE_BLOCKS

    def finish(cols, acc):
        o_all = acc[:, :V_DIM] / acc[:, V_DIM:]
        o_ref[:, cols] = _diff_finish(o_all, lam, g_ref[...], lam_init).astype(o_ref.dtype)

    def step(start, width, diagonal):
        if diagonal:
            shape = (2 * tq, tq)
            q_chunk = (lax.broadcasted_iota(jnp.int32, shape, 0) % tq) // CHUNK
            k_chunk = lax.broadcasted_iota(jnp.int32, shape, 1) // CHUNK
            visible = k_chunk <= q_chunk
        scores = [jnp.dot(qq, kt_ref[cols, pl.ds(start, width)], preferred_element_type=F32)
                  for cols, qq in zip(heads, qqs)]
        for cols, s, m_sc, acc_sc in zip(heads, scores, m_scs, acc_scs):
            v = v_ref[pl.ds(start, width), cols]
            if diagonal:
                last = jnp.where(visible, s[:, width - tq:], NEG_INF)
                s = last if width == tq else jnp.concatenate([s[:, :width - tq], last], axis=1)
                m_new = _row_max(s)
            else:
                m_prev = m_sc[...]
                m_new = jnp.maximum(m_prev, _row_max(s))
            p = jnp.exp2(s - _wide(m_new, width)).astype(BF16)
            pv = jnp.dot(p, _with_ones(v), preferred_element_type=F32)
            if diagonal and single_step:
                finish(cols, pv)
                continue
            if diagonal:
                acc_sc[...] = pv
            else:
                acc_sc[...] = _wide(jnp.exp2(m_prev - m_new), 2 * V_DIM) * acc_sc[...] + pv
            m_sc[...] = m_new

    for extra in range(min(WIDE_BLOCKS, n_q)):
        @pl.when(qi % WIDE_BLOCKS == extra)
        def _():
            step(pl.multiple_of((qi - extra) * tq, tq), (extra + 1) * tq, True)

    if not single_step:
        def body(kb, carry):
            step(pl.multiple_of(kb * WIDE_BLOCKS * tq, WIDE_BLOCKS * tq), WIDE_BLOCKS * tq, False)
            return carry

        lax.fori_loop(0, qi // WIDE_BLOCKS, body, 0)
        for cols, acc_sc in zip(heads, acc_scs):
            finish(cols, acc_sc[...])


def _attn_prompt(q, kt, v, attn_params, layer, batch, seq, tq):
    nq = seq // tq
    kt_spec = pl.BlockSpec((None, D_ATTN, seq), lambda b, i: (b, 0, 0))
    v_spec = pl.BlockSpec((seq, D_ATTN), lambda b, i: (b, 0))
    q_spec = pl.BlockSpec((tq, D_ATTN), lambda b, i: (b * nq + i, 0))
    return pl.pallas_call(
        functools.partial(_attn_prompt_kernel, lam_init=_lambda_init(layer), tq=tq, n_q=nq),
        grid=(batch, nq),
        in_specs=[_layer_spec(p, layer) for p in attn_params] + [q_spec, kt_spec, v_spec],
        out_specs=q_spec,
        out_shape=jax.ShapeDtypeStruct((batch * seq, D_ATTN), BF16),
        scratch_shapes=([pltpu.VMEM((2 * tq, LANES), F32)] * N_DIFF_HEADS
                        + [pltpu.VMEM((2 * tq, 2 * V_DIM), F32)] * N_DIFF_HEADS),
        compiler_params=_params(2),
        name="attn_prompt",
    )(*attn_params, q, kt, v)


def _attn_sample_kernel(lq1, lk1, lq2, lk2, g_ref, q_ref, kn_ref, vn_ref, ckt_ref, cv_ref, o_ref,
                        *, lam_init, past):
    lam = _lambda(lq1, lk1, lq2, lk2, lam_init)
    for h in range(N_DIFF_HEADS):
        cols = slice(h * V_DIM, (h + 1) * V_DIM)
        qq = _stack_sub_heads(q_ref[:, cols])
        s_past = jnp.dot(qq, ckt_ref[cols, :].astype(BF16), preferred_element_type=F32)
        s_new = _dot_nt(qq, kn_ref[:, cols])
        m = jnp.maximum(_row_max(s_past), jnp.max(s_new, axis=1, keepdims=True))
        p_past = jnp.exp2(s_past - _wide(m, past)).astype(BF16)
        p_new = jnp.exp2(s_new - m[:, :s_new.shape[1]]).astype(BF16)
        v_past = cv_ref[pl.ds(h, past, stride=N_DIFF_HEADS), :].astype(BF16)
        acc = (jnp.dot(p_past, _with_ones(v_past), preferred_element_type=F32)
               + jnp.dot(p_new, _with_ones(vn_ref[:, cols]), preferred_element_type=F32))
        o_all = acc[:, :V_DIM] / acc[:, V_DIM:]
        o_ref[:, cols] = _diff_finish(o_all, lam, g_ref[...], lam_init).astype(o_ref.dtype)


def _attn_sample(q, k, v, cache_kt, cache_v, attn_params, layer, batch, seq, past):
    new_spec = pl.BlockSpec((seq, D_ATTN), lambda b: (b, 0))
    return pl.pallas_call(
        functools.partial(_attn_sample_kernel, lam_init=_lambda_init(layer), past=past),
        grid=(batch,),
        in_specs=[_layer_spec(p, layer) for p in attn_params]
                 + [new_spec, new_spec, new_spec,
                    pl.BlockSpec((None, None, D_ATTN, past), lambda b: (layer, b, 0, 0)),
                    pl.BlockSpec((None, None, past * N_DIFF_HEADS, V_DIM), lambda b: (layer, b, 0, 0))],
        out_specs=new_spec,
        out_shape=jax.ShapeDtypeStruct((batch * seq, D_ATTN), BF16),
        compiler_params=_params(1),
        name="attn_sample",
    )(*attn_params, q, k, v, cache_kt, cache_v)


def _post_kernel(attn_ref, c_ref, x_ref, wo_ref, g1_ref, b1_ref, w1_ref, w2_ref, g2_ref, b2_ref, o_ref):
    tm = x_ref.shape[0]
    halves = [slice(i * (tm // POST_CHAINS), (i + 1) * (tm // POST_CHAINS)) for i in range(POST_CHAINS)]
    n_ff = D_FF // FF_CHUNK

    def out_proj(rows):
        return (jnp.dot(attn_ref[rows, :], wo_ref[0:D_ATTN, :], preferred_element_type=F32)
                + jnp.dot(c_ref[rows, :], wo_ref[D_ATTN:D_ATTN + D_CONV, :], preferred_element_type=F32))

    def ff(c, xb, acc):
        h = jnp.dot(xb, w1_ref[:, c * FF_CHUNK:(c + 1) * FF_CHUNK], preferred_element_type=F32)
        h = jnp.maximum(h, 0.0)
        h = (h * h).astype(BF16)
        return acc + jnp.dot(h, w2_ref[c * FF_CHUNK:(c + 1) * FF_CHUNK, :], preferred_element_type=F32)

    ys = [out_proj(rows) for rows in halves]
    x1s = [_layer_norm(DEEPNORM_ALPHA * x_ref[rows, :] + y, g1_ref[...], b1_ref[...])
           for rows, y in zip(halves, ys)]
    xbs = [x1.astype(BF16) for x1 in x1s]
    accs = [jnp.zeros(x1.shape, F32) for x1 in x1s]
    for c in range(n_ff):
        for i in range(POST_CHAINS):
            accs[i] = ff(c, xbs[i], accs[i])
            if c == n_ff - 1:
                o_ref[halves[i], :] = _layer_norm(DEEPNORM_ALPHA * x1s[i] + accs[i], g2_ref[...], b2_ref[...])


def _post(attn, c, x, post_params, layer, tm):
    n = x.shape[0]
    tok = lambda width: pl.BlockSpec((tm, width), lambda i: (i, 0))
    return pl.pallas_call(
        _post_kernel,
        grid=(n // tm,),
        in_specs=[tok(D_ATTN), tok(D_CONV), tok(D_MODEL)] + [_layer_spec(p, layer) for p in post_params],
        out_specs=tok(D_MODEL),
        out_shape=jax.ShapeDtypeStruct((n, D_MODEL), F32),
        compiler_params=_params(1),
        name="post",
    )(attn, c, x, *post_params)


def _rope_angles(pos):
    inv = ROPE_THETA ** (-jnp.arange(HALF, dtype=F32) / HALF)
    ang = pos.astype(F32)[:, None] * inv[None, :]
    return jnp.cos(ang), jnp.sin(ang)


def _rope_tables(pos):
    cos, sin = _rope_angles(pos)
    return (jnp.concatenate([cos, cos, cos, cos], axis=1),
            jnp.concatenate([-sin, sin, -sin, sin], axis=1))


def _rows(a):
    return a.reshape(a.shape[0], 1, a.shape[1])


TM = 512
TQ = 256
WIDE_BLOCKS = 8
POST_CHAINS = 2


def kernel(x_prompt, x_sample, cache_k, cache_v, cache_conv, w_in, lambda_q1, lambda_k1, lambda_q2,
           lambda_k2, subln_g, conv_w, conv_b, conv_ln_g, conv_ln_b, w_out, ln1_g, ln1_b, w_ff1,
           w_ff2, ln2_g, ln2_b):
    B, T, _ = x_prompt.shape
    Bs, Ts, _ = x_sample.shape
    P = cache_k.shape[2]
    depth = w_in.shape[0]

    cols = [w_in[:, :, :3 * D_ATTN]]
    for lt in range(D_CONV // LANES):
        for c0 in (3 * D_ATTN, 3 * D_ATTN + D_CONV):
            cols.append(w_in[:, :, c0 + lt * LANES:c0 + (lt + 1) * LANES])
    w_in_b = jnp.concatenate(cols, axis=2).astype(BF16)
    wk_t = jnp.swapaxes(w_in_b[:, :, D_ATTN:2 * D_ATTN], 1, 2)
    w_out_b = w_out.astype(BF16)
    w_ff1_b = w_ff1.astype(BF16)
    w_ff2_b = w_ff2.astype(BF16)

    pos_p = jnp.arange(T)
    cos_p, sin_p = _rope_tables(pos_p)
    cos_pt, sin_pt = (a.T for a in _rope_angles(pos_p))
    cos_s, sin_s = _rope_tables(P + jnp.arange(Ts))
    reps = TM // Ts
    cos_s = jnp.tile(cos_s, (reps, 1))
    sin_s = jnp.tile(sin_s, (reps, 1))

    xp = x_prompt.reshape(B * T, D_MODEL)
    xs = x_sample.reshape(Bs * Ts, D_MODEL)
    cache_kt = jnp.transpose(cache_k, (0, 1, 3, 4, 2)).reshape(depth, Bs, D_ATTN, P)
    cache_vr = cache_v.reshape(depth, Bs, P * N_DIFF_HEADS, V_DIM)
    hist_s = jnp.pad(cache_conv, ((0, 0), (0, 0), (HALO - CONV_HIST, 0), (0, 0)))
    hist_s = hist_s.reshape(depth, Bs * HALO, D_CONV)

    attn_params = tuple(_rows(a) for a in (lambda_q1, lambda_k1, lambda_q2, lambda_k2, subln_g))
    conv_params = (conv_w, _rows(conv_b), _rows(conv_ln_g), _rows(conv_ln_b))
    post_params = (w_out_b, _rows(ln1_g), _rows(ln1_b), w_ff1_b, w_ff2_b, _rows(ln2_g), _rows(ln2_b))

    outs = {k: [] for k in ("cp", "ks", "vs", "cs")}
    kv_prompt = None
    for l in range(depth):
        q, ktf, ktb, vf, vb, c, tail = _in_proj_conv(xp, w_in_b, wk_t, cos_p, sin_p, cos_pt, sin_pt,
                                                     conv_params, l, B, T, TM, kv_prompt)
        kv_prompt = (ktf, vf)
        attn = _attn_prompt(q, ktb, vb, attn_params, l, B, T, TQ)
        xp = _post(attn, c, xp, post_params, l, TM)
        outs["cp"].append(tail[:, HALO - CONV_HIST:])

        q, kf, kb, vf_s, vb, u = _in_proj_rows(xs, w_in_b, cos_s, sin_s, l, TM)
        attn = _attn_sample(q, kb, vb, cache_kt, cache_vr, attn_params, l, Bs, Ts, P)
        c = _conv_rows(u, hist_s, conv_params, l, Ts, TM)
        xs = _post(attn, c, xs, post_params, l, TM)
        outs["ks"].append(kf.reshape(Bs, Ts, N_SUB_HEADS, HEAD_DIM))
        outs["vs"].append(vf_s.reshape(Bs, Ts, N_DIFF_HEADS, V_DIM))
        outs["cs"].append(u.reshape(Bs, Ts, D_CONV)[:, Ts - CONV_HIST:])

    ktf, vf = kv_prompt
    new_k_prompt = jnp.transpose(ktf.reshape(depth, B, N_SUB_HEADS, HEAD_DIM, T), (0, 1, 4, 2, 3))
    new_v_prompt = vf.reshape(depth, B, T, N_DIFF_HEADS, V_DIM)
    return (xp.reshape(B, T, D_MODEL), xs.reshape(Bs, Ts, D_MODEL),
            new_k_prompt, new_v_prompt, jnp.stack(outs["cp"]),
            jnp.stack(outs["ks"]), jnp.stack(outs["vs"]), jnp.stack(outs["cs"]))
```

```python
import functools
import math

import jax
import jax.numpy as jnp
from jax import lax
from jax.experimental import pallas as pl
from jax.experimental.pallas import tpu as pltpu

D_MODEL = 1024
D_ATTN = 512
D_CONV = 512
N_DIFF_HEADS = 4
N_SUB_HEADS = 8
HEAD_DIM = 64
HALF = HEAD_DIM // 2
V_DIM = 128
LANES = 128
SUBLANES = 8
CHUNK = 64
CONV_WIDTH = 31
CONV_HIST = CONV_WIDTH - 1
HALO = 32
CONV_ROWS = 64
TILE_CONV_ROWS = 512
D_FF = 4096
FF_CHUNK = 1024
ROPE_THETA = 10000.0
LN_EPS = 1e-5
RMS_EPS = 1e-5
DEPTH = 2
DEEPNORM_ALPHA = (2.0 * DEPTH) ** 0.25
NEG_INF = -1e30
LOG2E = math.log2(math.e)
Q_SCALE = HEAD_DIM ** -0.5 * LOG2E
VMEM_LIMIT_BYTES = 56 * 1024 * 1024

BF16 = jnp.bfloat16
F32 = jnp.float32


def _lambda_init(layer):
    return 0.8 - 0.6 * math.exp(-0.3 * layer)


def _params(n_axes):
    return pltpu.CompilerParams(dimension_semantics=("arbitrary",) * n_axes,
                                vmem_limit_bytes=VMEM_LIMIT_BYTES)


def _layer_spec(param, layer):
    rest = param.shape[1:]
    return pl.BlockSpec((None,) + rest, lambda *_: (layer,) + (0,) * len(rest),
                        pipeline_mode=pl.Buffered(1))


def _layer_norm(z, g, b):
    mu = jnp.mean(z, axis=-1, keepdims=True)
    d = z - mu
    var = jnp.mean(d * d, axis=-1, keepdims=True)
    return d * lax.rsqrt(var + LN_EPS) * g + b


def _dot_nt(a, b):
    return lax.dot_general(a, b, (((1,), (1,)), ((), ())), preferred_element_type=F32)


def _conv_chunk(win_ref, base, cw_ref, cb_ref, cg_ref, cbeta_ref):
    tiles = [_conv_lanes(win_ref, base, cw_ref, slice(lt * LANES, (lt + 1) * LANES))
             for lt in range(D_CONV // LANES)]
    return _conv_finish(jnp.concatenate(tiles, axis=1), cb_ref, cg_ref, cbeta_ref)


def _conv_lanes(win_ref, base, cw_ref, lanes, n_rows=CONV_ROWS):
    off = HALO - CONV_HIST
    x = win_ref[pl.ds(base, n_rows + HALO), lanes]
    acc = None
    for r in range(SUBLANES):
        rows = n_rows + SUBLANES if r else n_rows
        part = None
        for a in range((off + CONV_WIDTH - 1 - r) // SUBLANES + 1):
            j = SUBLANES * a + r - off
            if j < 0:
                continue
            term = cw_ref[j:j + 1, lanes] * x[SUBLANES * a:SUBLANES * a + rows]
            part = term if part is None else part + term
        if r:
            part = pltpu.roll(part, rows - r, 0)[:n_rows]
        acc = part if acc is None else acc + part
    return acc


def _conv_finish(c, cb_ref, cg_ref, cbeta_ref):
    c = _layer_norm(c + cb_ref[...], cg_ref[...], cbeta_ref[...])
    return c * jax.nn.sigmoid(c)


def _conv_rows_kernel(u_ref, halo_ref, cw_ref, cb_ref, cg_ref, cbeta_ref, c_ref, win_sc, *, seq):
    for b in range(u_ref.shape[0] // seq):
        win = win_sc.at[b]
        win[0:HALO, :] = halo_ref[b * HALO:(b + 1) * HALO, :]
        win[HALO:HALO + seq, :] = u_ref[b * seq:(b + 1) * seq, :]
        for rc in range(seq // CONV_ROWS):
            base = rc * CONV_ROWS
            c_ref[b * seq + base:b * seq + base + CONV_ROWS, :] = _conv_chunk(
                win, base, cw_ref, cb_ref, cg_ref, cbeta_ref).astype(BF16)


def _conv_rows(u, halo, conv_params, layer, seq, tm):
    n = u.shape[0]
    per_tile = tm // seq
    return pl.pallas_call(
        functools.partial(_conv_rows_kernel, seq=seq),
        grid=(n // tm,),
        in_specs=[pl.BlockSpec((tm, D_CONV), lambda i: (i, 0)),
                  pl.BlockSpec((None, per_tile * HALO, D_CONV), lambda i: (layer, i, 0))]
                 + [_layer_spec(p, layer) for p in conv_params],
        out_specs=pl.BlockSpec((tm, D_CONV), lambda i: (i, 0)),
        out_shape=jax.ShapeDtypeStruct((n, D_CONV), BF16),
        scratch_shapes=[pltpu.VMEM((per_tile, HALO + seq, D_CONV), F32)],
        compiler_params=_params(1),
        name="conv_rows",
    )(u, halo, *conv_params)


def _rope_rows(p, cos_ref, sin_ref):
    n = p.shape[1] // LANES
    cos = jnp.concatenate([cos_ref[...]] * n, axis=1)
    sin = jnp.concatenate([sin_ref[...]] * n, axis=1)
    lane = lax.broadcasted_iota(jnp.int32, p.shape, 1)
    partner = jnp.where((lane % HEAD_DIM) < HALF,
                        pltpu.roll(p, p.shape[1] - HALF, 1),
                        pltpu.roll(p, HALF, 1))
    return p * cos + partner * sin


def _rope_cols(pt, cos_ref, sin_ref):
    cos = cos_ref[...]
    sin = sin_ref[...]
    parts = []
    for s in range(pt.shape[0] // HEAD_DIM):
        a = pt[s * HEAD_DIM:s * HEAD_DIM + HALF]
        b = pt[s * HEAD_DIM + HALF:(s + 1) * HEAD_DIM]
        parts += [a * cos - b * sin, b * cos + a * sin]
    return jnp.concatenate(parts, axis=0)


def _proj(xb, w_ref, c0, width):
    return jnp.dot(xb, w_ref[:, c0:c0 + width], preferred_element_type=F32)


def _glu_lanes(xb, w_ref, lt):
    r = _proj(xb, w_ref, 3 * D_ATTN + 2 * LANES * lt, 2 * LANES)
    return r[:, :LANES] * jax.nn.sigmoid(r[:, LANES:])


def _glu(xb, w_ref):
    return jnp.concatenate([_glu_lanes(xb, w_ref, lt) for lt in range(D_CONV // LANES)], axis=1)


def _in_proj_rows_kernel(x_ref, w_ref, cos_ref, sin_ref, q_ref, kf_ref, kb_ref, vf_ref, vb_ref, u_ref):
    xb = x_ref[...].astype(BF16)
    q_ref[...] = (_rope_rows(_proj(xb, w_ref, 0, D_ATTN), cos_ref, sin_ref) * Q_SCALE).astype(BF16)
    k = _rope_rows(_proj(xb, w_ref, D_ATTN, D_ATTN), cos_ref, sin_ref)
    kf_ref[...] = k
    kb_ref[...] = k.astype(BF16)
    v = _proj(xb, w_ref, 2 * D_ATTN, D_ATTN)
    vf_ref[...] = v
    vb_ref[...] = v.astype(BF16)
    u_ref[...] = _glu(xb, w_ref)


def _in_proj_rows(x, w_in, cos, sin, layer, tm):
    n = x.shape[0]
    n_tab = cos.shape[0] // tm
    tok = lambda width: pl.BlockSpec((tm, width), lambda i: (i, 0))
    tab = pl.BlockSpec((tm, LANES), lambda i: (i % n_tab, 0))
    return pl.pallas_call(
        _in_proj_rows_kernel,
        grid=(n // tm,),
        in_specs=[tok(D_MODEL), _layer_spec(w_in, layer), tab, tab],
        out_specs=[tok(D_ATTN)] * 5 + [tok(D_CONV)],
        out_shape=[jax.ShapeDtypeStruct((n, D_ATTN), BF16),
                   jax.ShapeDtypeStruct((n, D_ATTN), F32),
                   jax.ShapeDtypeStruct((n, D_ATTN), BF16),
                   jax.ShapeDtypeStruct((n, D_ATTN), F32),
                   jax.ShapeDtypeStruct((n, D_ATTN), BF16),
                   jax.ShapeDtypeStruct((n, D_CONV), F32)],
        compiler_params=_params(1),
        name="in_proj_rows",
    )(x, w_in, cos, sin)


def _in_proj_conv_kernel(x_ref, w_ref, wkt_ref, cos_ref, sin_ref, cost_ref, sint_ref,
                         cw_ref, cb_ref, cg_ref, cbeta_ref, *rest, tiles_per_seq):
    q_ref, ktf_ref, ktb_ref, vf_ref, vb_ref, c_ref, tail_ref, win_sc, pre_sc = rest[-9:]
    tm = x_ref.shape[0]
    first = (pl.program_id(0) % tiles_per_seq) == 0

    @pl.when(first)
    def _():
        win_sc[0:HALO, :] = jnp.zeros((HALO, D_CONV), F32)

    @pl.when(jnp.logical_not(first))
    def _():
        win_sc[0:HALO, :] = win_sc[tm:tm + HALO, :]

    xb = x_ref[...].astype(BF16)

    def proj_q():
        q_ref[...] = _rope_rows(_proj(xb, w_ref, 0, D_ATTN), cos_ref, sin_ref).astype(BF16)

    def proj_k():
        kt = _rope_cols(_dot_nt(wkt_ref[...], xb), cost_ref, sint_ref)
        ktf_ref[...] = kt
        ktb_ref[...] = kt.astype(BF16)

    def proj_v():
        v = _proj(xb, w_ref, 2 * D_ATTN, D_ATTN)
        vb_ref[...] = v.astype(BF16)
        for h in range(N_DIFF_HEADS):
            vf_ref[pl.ds(h, tm, stride=N_DIFF_HEADS), :] = v[:, h * V_DIM:(h + 1) * V_DIM]

    projections = {1: (proj_q, proj_k), 2: (proj_v,)}
    for lt in range(D_CONV // LANES):
        lanes = slice(lt * LANES, (lt + 1) * LANES)
        u = _glu_lanes(xb, w_ref, lt)
        win_sc[HALO:HALO + tm, lanes] = u
        tail_ref[:, lanes] = u[tm - HALO:tm]
        for proj in projections.get(lt, ()):
            proj()
        for base in range(0, tm, TILE_CONV_ROWS):
            pre_sc[base:base + TILE_CONV_ROWS, lanes] = _conv_lanes(win_sc, base, cw_ref, lanes,
                                                                    TILE_CONV_ROWS)
    for rc in range(tm // CONV_ROWS):
        rows = slice(rc * CONV_ROWS, (rc + 1) * CONV_ROWS)
        c_ref[rows, :] = _conv_finish(pre_sc[rows, :], cb_ref, cg_ref, cbeta_ref).astype(BF16)


def _in_proj_conv(x, w_in, wk_t, cos, sin, cos_t, sin_t, conv_params, layer, batch, seq, tm, prev):
    n = x.shape[0]
    nt = seq // tm
    depth = w_in.shape[0]
    tok = lambda width: pl.BlockSpec((tm, width), lambda i: (i, 0))
    tab = pl.BlockSpec((tm, LANES), lambda i: (i % nt, 0))
    tab_t = pl.BlockSpec((HALF, tm), lambda i: (0, i % nt))
    in_specs = ([tok(D_MODEL), _layer_spec(w_in, layer), _layer_spec(wk_t, layer), tab, tab, tab_t, tab_t]
                + [_layer_spec(p, layer) for p in conv_params])
    args = [x, w_in, wk_t, cos, sin, cos_t, sin_t, *conv_params]
    aliases = {}
    if prev is not None:
        in_specs += [pl.BlockSpec(memory_space=pl.ANY)] * 2
        aliases = {len(args): 1, len(args) + 1: 3}
        args += list(prev)
    out_specs = [tok(D_ATTN),
                 pl.BlockSpec((None, None, D_ATTN, tm), lambda i: (layer, i // nt, 0, i % nt)),
                 pl.BlockSpec((None, D_ATTN, tm), lambda i: (i // nt, 0, i % nt)),
                 pl.BlockSpec((None, tm * N_DIFF_HEADS, V_DIM), lambda i: (layer, i, 0)),
                 tok(D_ATTN), tok(D_CONV),
                 pl.BlockSpec((None, HALO, D_CONV), lambda i: (i // nt, 0, 0))]
    out_shape = [jax.ShapeDtypeStruct((n, D_ATTN), BF16),
                 jax.ShapeDtypeStruct((depth, batch, D_ATTN, seq), F32),
                 jax.ShapeDtypeStruct((batch, D_ATTN, seq), BF16),
                 jax.ShapeDtypeStruct((depth, n * N_DIFF_HEADS, V_DIM), F32),
                 jax.ShapeDtypeStruct((n, D_ATTN), BF16),
                 jax.ShapeDtypeStruct((n, D_CONV), BF16),
                 jax.ShapeDtypeStruct((batch, HALO, D_CONV), F32)]
    return pl.pallas_call(
        functools.partial(_in_proj_conv_kernel, tiles_per_seq=nt),
        grid=(n // tm,),
        in_specs=in_specs,
        out_specs=out_specs,
        out_shape=out_shape,
        input_output_aliases=aliases,
        scratch_shapes=[pltpu.VMEM((HALO + tm, D_CONV), F32), pltpu.VMEM((tm, D_CONV), F32)],
        compiler_params=_params(1),
        name="in_proj_conv",
    )(*args)


def _stack_sub_heads(q):
    lane = lax.broadcasted_iota(jnp.int32, q.shape, 1)
    zero = jnp.zeros_like(q)
    return jnp.concatenate([jnp.where(lane < HEAD_DIM, q, zero),
                            jnp.where(lane >= HEAD_DIM, q, zero)], axis=0)


def _lambda(lq1, lk1, lq2, lk2, lam_init):
    s1 = jnp.sum(lq1[...] * lk1[...], axis=1, keepdims=True)
    s2 = jnp.sum(lq2[...] * lk2[...], axis=1, keepdims=True)
    return jnp.exp(s1) - jnp.exp(s2) + lam_init


def _diff_finish(o_all, lam, g, lam_init):
    t = o_all.shape[0] // 2
    o = o_all[:t] - lam * o_all[t:]
    o = o * lax.rsqrt(jnp.mean(o * o, axis=-1, keepdims=True) + RMS_EPS)
    return o * g * (1.0 - lam_init)


def _row_max(s):
    m = s[:, 0:LANES]
    for c in range(1, s.shape[1] // LANES):
        m = jnp.maximum(m, s[:, c * LANES:(c + 1) * LANES])
    return jnp.broadcast_to(jnp.max(m, axis=1, keepdims=True), m.shape)


def _with_ones(v):
    return jnp.concatenate([v, jnp.ones(v.shape, v.dtype)], axis=1)


def _wide(a, width):
    return jnp.concatenate([a] * (width // LANES), axis=1)


def _attn_prompt_kernel(lq1, lk1, lq2, lk2, g_ref, q_ref, kt_ref, v_ref, o_ref, *scratch,
                        lam_init, tq, n_q):
    m_scs = scratch[:N_DIFF_HEADS]
    acc_scs = scratch[N_DIFF_HEADS:]
    qi = pl.program_id(1)
    heads = [slice(h * V_DIM, (h + 1) * V_DIM) for h in range(N_DIFF_HEADS)]
    qqs = [_stack_sub_heads(q_ref[:, cols]) for cols in heads]
    lam = _lambda(lq1, lk1, lq2, lk2, lam_init)
    single_step = n_q <= WIDE_BLOCKS

    def finish(cols, acc):
        o_all = acc[:, :V_DIM] / acc[:, V_DIM:]
        o_ref[:, cols] = _diff_finish(o_all, lam, g_ref[...], lam_init).astype(o_ref.dtype)

    def step(start, width, diagonal):
        if diagonal:
            shape = (2 * tq, tq)
            q_chunk = (lax.broadcasted_iota(jnp.int32, shape, 0) % tq) // CHUNK
            k_chunk = lax.broadcasted_iota(jnp.int32, shape, 1) // CHUNK
            visible = k_chunk <= q_chunk
        scores = [jnp.dot(qq, kt_ref[cols, pl.ds(start, width)], preferred_element_type=F32)
                  for cols, qq in zip(heads, qqs)]
        for cols, s, m_sc, acc_sc in zip(heads, scores, m_scs, acc_scs):
            v = v_ref[pl.ds(start, width), cols]
            if diagonal:
                last = jnp.where(visible, s[:, width - tq:], NEG_INF)
                s = last if width == tq else jnp.concatenate([s[:, :width - tq], last], axis=1)
                m_new = _row_max(s)
            else:
                m_prev = m_sc[...]
                m_new = jnp.maximum(m_prev, _row_max(s))
            p = jnp.exp2(s - _wide(m_new, width)).astype(BF16)
            pv = jnp.dot(p, _with_ones(v), preferred_element_type=F32)
            if diagonal and single_step:
                finish(cols, pv)
                continue
            if diagonal:
                acc_sc[...] = pv
            else:
                acc_sc[...] = _wide(jnp.exp2(m_prev - m_new), 2 * V_DIM) * acc_sc[...] + pv
            m_sc[...] = m_new

    for extra in range(min(WIDE_BLOCKS, n_q)):
        @pl.when(qi % WIDE_BLOCKS == extra)
        def _():
            step(pl.multiple_of((qi - extra) * tq, tq), (extra + 1) * tq, True)

    if not single_step:
        def body(kb, carry):
            step(pl.multiple_of(kb * WIDE_BLOCKS * tq, WIDE_BLOCKS * tq), WIDE_BLOCKS * tq, False)
            return carry

        lax.fori_loop(0, qi // WIDE_BLOCKS, body, 0)
        for cols, acc_sc in zip(heads, acc_scs):
            finish(cols, acc_sc[...])


def _attn_prompt(q, kt, v, attn_params, layer, batch, seq, tq):
    nq = seq // tq
    kt_spec = pl.BlockSpec((None, D_ATTN, seq), lambda b, i: (b, 0, 0))
    v_spec = pl.BlockSpec((seq, D_ATTN), lambda b, i: (b, 0))
    q_spec = pl.BlockSpec((tq, D_ATTN), lambda b, i: (b * nq + i, 0))
    return pl.pallas_call(
        functools.partial(_attn_prompt_kernel, lam_init=_lambda_init(layer), tq=tq, n_q=nq),
        grid=(batch, nq),
        in_specs=[_layer_spec(p, layer) for p in attn_params] + [q_spec, kt_spec, v_spec],
        out_specs=q_spec,
        out_shape=jax.ShapeDtypeStruct((batch * seq, D_ATTN), BF16),
        scratch_shapes=([pltpu.VMEM((2 * tq, LANES), F32)] * N_DIFF_HEADS
                        + [pltpu.VMEM((2 * tq, 2 * V_DIM), F32)] * N_DIFF_HEADS),
        compiler_params=_params(2),
        name="attn_prompt",
    )(*attn_params, q, kt, v)


def _attn_sample_kernel(lq1, lk1, lq2, lk2, g_ref, q_ref, kn_ref, vn_ref, ckt_ref, cv_ref, o_ref,
                        *, lam_init, past):
    lam = _lambda(lq1, lk1, lq2, lk2, lam_init)
    for h in range(N_DIFF_HEADS):
        cols = slice(h * V_DIM, (h + 1) * V_DIM)
        qq = _stack_sub_heads(q_ref[:, cols])
        s_past = jnp.dot(qq, ckt_ref[cols, :].astype(BF16), preferred_element_type=F32)
        s_new = _dot_nt(qq, kn_ref[:, cols])
        m = jnp.maximum(_row_max(s_past), jnp.max(s_new, axis=1, keepdims=True))
        p_past = jnp.exp2(s_past - _wide(m, past)).astype(BF16)
        p_new = jnp.exp2(s_new - m[:, :s_new.shape[1]]).astype(BF16)
        v_past = cv_ref[pl.ds(h, past, stride=N_DIFF_HEADS), :].astype(BF16)
        acc = (jnp.dot(p_past, _with_ones(v_past), preferred_element_type=F32)
               + jnp.dot(p_new, _with_ones(vn_ref[:, cols]), preferred_element_type=F32))
        o_all = acc[:, :V_DIM] / acc[:, V_DIM:]
        o_ref[:, cols] = _diff_finish(o_all, lam, g_ref[...], lam_init).astype(o_ref.dtype)


def _attn_sample(q, k, v, cache_kt, cache_v, attn_params, layer, batch, seq, past):
    new_spec = pl.BlockSpec((seq, D_ATTN), lambda b: (b, 0))
    return pl.pallas_call(
        functools.partial(_attn_sample_kernel, lam_init=_lambda_init(layer), past=past),
        grid=(batch,),
        in_specs=[_layer_spec(p, layer) for p in attn_params]
                 + [new_spec, new_spec, new_spec,
                    pl.BlockSpec((None, None, D_ATTN, past), lambda b: (layer, b, 0, 0)),
                    pl.BlockSpec((None, None, past * N_DIFF_HEADS, V_DIM), lambda b: (layer, b, 0, 0))],
        out_specs=new_spec,
        out_shape=jax.ShapeDtypeStruct((batch * seq, D_ATTN), BF16),
        compiler_params=_params(1),
        name="attn_sample",
    )(*attn_params, q, k, v, cache_kt, cache_v)


def _post_kernel(attn_ref, c_ref, x_ref, wo_ref, g1_ref, b1_ref, w1_ref, w2_ref, g2_ref, b2_ref, o_ref):
    tm = x_ref.shape[0]
    halves = [slice(i * (tm // POST_CHAINS), (i + 1) * (tm // POST_CHAINS)) for i in range(POST_CHAINS)]
    n_ff = D_FF // FF_CHUNK

    def out_proj(rows):
        return (jnp.dot(attn_ref[rows, :], wo_ref[0:D_ATTN, :], preferred_element_type=F32)
                + jnp.dot(c_ref[rows, :], wo_ref[D_ATTN:D_ATTN + D_CONV, :], preferred_element_type=F32))

    def ff(c, xb, acc):
        h = jnp.dot(xb, w1_ref[:, c * FF_CHUNK:(c + 1) * FF_CHUNK], preferred_element_type=F32)
        h = jnp.maximum(h, 0.0)
        h = (h * h).astype(BF16)
        return acc + jnp.dot(h, w2_ref[c * FF_CHUNK:(c + 1) * FF_CHUNK, :], preferred_element_type=F32)

    ys = [out_proj(rows) for rows in halves]
    x1s = [_layer_norm(DEEPNORM_ALPHA * x_ref[rows, :] + y, g1_ref[...], b1_ref[...])
           for rows, y in zip(halves, ys)]
    xbs = [x1.astype(BF16) for x1 in x1s]
    accs = [jnp.zeros(x1.shape, F32) for x1 in x1s]
    for c in range(n_ff):
        for i in range(POST_CHAINS):
            accs[i] = ff(c, xbs[i], accs[i])
            if c == n_ff - 1:
                o_ref[halves[i], :] = _layer_norm(DEEPNORM_ALPHA * x1s[i] + accs[i], g2_ref[...], b2_ref[...])


def _post(attn, c, x, post_params, layer, tm):
    n = x.shape[0]
    tok = lambda width: pl.BlockSpec((tm, width), lambda i: (i, 0))
    return pl.pallas_call(
        _post_kernel,
        grid=(n // tm,),
        in_specs=[tok(D_ATTN), tok(D_CONV), tok(D_MODEL)] + [_layer_spec(p, layer) for p in post_params],
        out_specs=tok(D_MODEL),
        out_shape=jax.ShapeDtypeStruct((n, D_MODEL), F32),
        compiler_params=_params(1),
        name="post",
    )(attn, c, x, *post_params)


def _rope_angles(pos):
    inv = ROPE_THETA ** (-jnp.arange(HALF, dtype=F32) / HALF)
    ang = pos.astype(F32)[:, None] * inv[None, :]
    return jnp.cos(ang), jnp.sin(ang)


def _rope_tables(pos):
    cos, sin = _rope_angles(pos)
    return (jnp.concatenate([cos, cos, cos, cos], axis=1),
            jnp.concatenate([-sin, sin, -sin, sin], axis=1))


def _rows(a):
    return a.reshape(a.shape[0], 1, a.shape[1])


TM = 512
TQ = 256
WIDE_BLOCKS = 8
POST_CHAINS = 2


def kernel(x_prompt, x_sample, cache_k, cache_v, cache_conv, w_in, lambda_q1, lambda_k1, lambda_q2,
           lambda_k2, subln_g, conv_w, conv_b, conv_ln_g, conv_ln_b, w_out, ln1_g, ln1_b, w_ff1,
           w_ff2, ln2_g, ln2_b):
    B, T, _ = x_prompt.shape
    Bs, Ts, _ = x_sample.shape
    P = cache_k.shape[2]
    depth = w_in.shape[0]

    cols = [w_in[:, :, :3 * D_ATTN]]
    for lt in range(D_CONV // LANES):
        for c0 in (3 * D_ATTN, 3 * D_ATTN + D_CONV):
            cols.append(w_in[:, :, c0 + lt * LANES:c0 + (lt + 1) * LANES])
    w_in_b = jnp.concatenate(cols, axis=2).astype(BF16)
    wk_t = jnp.swapaxes(w_in_b[:, :, D_ATTN:2 * D_ATTN], 1, 2)
    w_out_b = w_out.astype(BF16)
    w_ff1_b = w_ff1.astype(BF16)
    w_ff2_b = w_ff2.astype(BF16)

    pos_p = jnp.arange(T)
    cos_p, sin_p = (a * Q_SCALE for a in _rope_tables(pos_p))
    cos_pt, sin_pt = (a.T for a in _rope_angles(pos_p))
    cos_s, sin_s = _rope_tables(P + jnp.arange(Ts))
    reps = TM // Ts
    cos_s = jnp.tile(cos_s, (reps, 1))
    sin_s = jnp.tile(sin_s, (reps, 1))

    xp = x_prompt.reshape(B * T, D_MODEL)
    xs = x_sample.reshape(Bs * Ts, D_MODEL)
    cache_kt = jnp.transpose(cache_k, (0, 1, 3, 4, 2)).reshape(depth, Bs, D_ATTN, P)
    cache_vr = cache_v.reshape(depth, Bs, P * N_DIFF_HEADS, V_DIM)
    hist_s = jnp.pad(cache_conv, ((0, 0), (0, 0), (HALO - CONV_HIST, 0), (0, 0)))
    hist_s = hist_s.reshape(depth, Bs * HALO, D_CONV)

    attn_params = tuple(_rows(a) for a in (lambda_q1, lambda_k1, lambda_q2, lambda_k2, subln_g))
    conv_params = (conv_w, _rows(conv_b), _rows(conv_ln_g), _rows(conv_ln_b))
    post_params = (w_out_b, _rows(ln1_g), _rows(ln1_b), w_ff1_b, w_ff2_b, _rows(ln2_g), _rows(ln2_b))

    outs = {k: [] for k in ("cp", "ks", "vs", "cs")}
    kv_prompt = None
    for l in range(depth):
        q, ktf, ktb, vf, vb, c, tail = _in_proj_conv(xp, w_in_b, wk_t, cos_p, sin_p, cos_pt, sin_pt,
                                                     conv_params, l, B, T, TM, kv_prompt)
        kv_prompt = (ktf, vf)
        attn = _attn_prompt(q, ktb, vb, attn_params, l, B, T, TQ)
        xp = _post(attn, c, xp, post_params, l, TM)
        outs["cp"].append(tail[:, HALO - CONV_HIST:])

        q, kf, kb, vf_s, vb, u = _in_proj_rows(xs, w_in_b, cos_s, sin_s, l, TM)
        attn = _attn_sample(q, kb, vb, cache_kt, cache_vr, attn_params, l, Bs, Ts, P)
        c = _conv_rows(u, hist_s, conv_params, l, Ts, TM)
        xs = _post(attn, c, xs, post_params, l, TM)
        outs["ks"].append(kf.reshape(Bs, Ts, N_SUB_HEADS, HEAD_DIM))
        outs["vs"].append(vf_s.reshape(Bs, Ts, N_DIFF_HEADS, V_DIM))
        outs["cs"].append(u.reshape(Bs, Ts, D_CONV)[:, Ts - CONV_HIST:])

    ktf, vf = kv_prompt
    new_k_prompt = jnp.transpose(ktf.reshape(depth, B, N_SUB_HEADS, HEAD_DIM, T), (0, 1, 4, 2, 3))
    new_v_prompt = vf.reshape(depth, B, T, N_DIFF_HEADS, V_DIM)
    return (xp.reshape(B, T, D_MODEL), xs.reshape(Bs, Ts, D_MODEL),
            new_k_prompt, new_v_prompt, jnp.stack(outs["cp"]),
            jnp.stack(outs["ks"]), jnp.stack(outs["vs"]), jnp.stack(outs["cs"]))
```

```python
import functools
import math

import jax
import jax.numpy as jnp
from jax import lax
from jax.experimental import pallas as pl
from jax.experimental.pallas import tpu as pltpu

D_MODEL = 1024
D_ATTN = 512
D_CONV = 512
N_DIFF_HEADS = 4
N_SUB_HEADS = 8
HEAD_DIM = 64
HALF = HEAD_DIM // 2
V_DIM = 128
LANES = 128
SUBLANES = 8
CHUNK = 64
CONV_WIDTH = 31
CONV_HIST = CONV_WIDTH - 1
HALO = 32
CONV_ROWS = 64
TILE_CONV_ROWS = 512
D_FF = 4096
FF_CHUNK = 1024
ROPE_THETA = 10000.0
LN_EPS = 1e-5
RMS_EPS = 1e-5
DEPTH = 2
DEEPNORM_ALPHA = (2.0 * DEPTH) ** 0.25
NEG_INF = -1e30
LOG2E = math.log2(math.e)
Q_SCALE = HEAD_DIM ** -0.5 * LOG2E
VMEM_LIMIT_BYTES = 56 * 1024 * 1024

BF16 = jnp.bfloat16
F32 = jnp.float32


def _lambda_init(layer):
    return 0.8 - 0.6 * math.exp(-0.3 * layer)


def _params(n_axes):
    return pltpu.CompilerParams(dimension_semantics=("arbitrary",) * n_axes,
                                vmem_limit_bytes=VMEM_LIMIT_BYTES)


def _layer_spec(param, layer):
    rest = param.shape[1:]
    return pl.BlockSpec((None,) + rest, lambda *_: (layer,) + (0,) * len(rest),
                        pipeline_mode=pl.Buffered(1))


def _layer_norm(z, g, b):
    mu = jnp.mean(z, axis=-1, keepdims=True)
    d = z - mu
    var = jnp.mean(d * d, axis=-1, keepdims=True)
    return d * lax.rsqrt(var + LN_EPS) * g + b


def _dot_nt(a, b):
    return lax.dot_general(a, b, (((1,), (1,)), ((), ())), preferred_element_type=F32)


def _conv_chunk(win_ref, base, cw_ref, cb_ref, cg_ref, cbeta_ref):
    tiles = [_conv_lanes(win_ref, base, cw_ref, slice(lt * LANES, (lt + 1) * LANES))
             for lt in range(D_CONV // LANES)]
    return _conv_finish(jnp.concatenate(tiles, axis=1), cb_ref, cg_ref, cbeta_ref)


def _conv_lanes(win_ref, base, cw_ref, lanes, n_rows=CONV_ROWS):
    off = HALO - CONV_HIST
    x = win_ref[pl.ds(base, n_rows + HALO), lanes]
    acc = None
    for r in range(SUBLANES):
        rows = n_rows + SUBLANES if r else n_rows
        part = None
        for a in range((off + CONV_WIDTH - 1 - r) // SUBLANES + 1):
            j = SUBLANES * a + r - off
            if j < 0:
                continue
            term = cw_ref[j:j + 1, lanes] * x[SUBLANES * a:SUBLANES * a + rows]
            part = term if part is None else part + term
        if r:
            part = pltpu.roll(part, rows - r, 0)[:n_rows]
        acc = part if acc is None else acc + part
    return acc


def _conv_finish(c, cb_ref, cg_ref, cbeta_ref):
    c = _layer_norm(c + cb_ref[...], cg_ref[...], cbeta_ref[...])
    return c * jax.nn.sigmoid(c)


def _conv_rows_kernel(u_ref, halo_ref, cw_ref, cb_ref, cg_ref, cbeta_ref, c_ref, win_sc, *, seq):
    for b in range(u_ref.shape[0] // seq):
        win = win_sc.at[b]
        win[0:HALO, :] = halo_ref[b * HALO:(b + 1) * HALO, :]
        win[HALO:HALO + seq, :] = u_ref[b * seq:(b + 1) * seq, :]
        for rc in range(seq // CONV_ROWS):
            base = rc * CONV_ROWS
            c_ref[b * seq + base:b * seq + base + CONV_ROWS, :] = _conv_chunk(
                win, base, cw_ref, cb_ref, cg_ref, cbeta_ref).astype(BF16)


def _conv_rows(u, halo, conv_params, layer, seq, tm):
    n = u.shape[0]
    per_tile = tm // seq
    return pl.pallas_call(
        functools.partial(_conv_rows_kernel, seq=seq),
        grid=(n // tm,),
        in_specs=[pl.BlockSpec((tm, D_CONV), lambda i: (i, 0)),
                  pl.BlockSpec((None, per_tile * HALO, D_CONV), lambda i: (layer, i, 0))]
                 + [_layer_spec(p, layer) for p in conv_params],
        out_specs=pl.BlockSpec((tm, D_CONV), lambda i: (i, 0)),
        out_shape=jax.ShapeDtypeStruct((n, D_CONV), BF16),
        scratch_shapes=[pltpu.VMEM((per_tile, HALO + seq, D_CONV), F32)],
        compiler_params=_params(1),
        name="conv_rows",
    )(u, halo, *conv_params)


def _rope_rows(p, cos_ref, sin_ref):
    n = p.shape[1] // LANES
    cos = jnp.concatenate([cos_ref[...]] * n, axis=1)
    sin = jnp.concatenate([sin_ref[...]] * n, axis=1)
    lane = lax.broadcasted_iota(jnp.int32, p.shape, 1)
    partner = jnp.where((lane % HEAD_DIM) < HALF,
                        pltpu.roll(p, p.shape[1] - HALF, 1),
                        pltpu.roll(p, HALF, 1))
    return p * cos + partner * sin


def _rope_cols(pt, cos_ref, sin_ref):
    cos = cos_ref[...]
    sin = sin_ref[...]
    parts = []
    for s in range(pt.shape[0] // HEAD_DIM):
        a = pt[s * HEAD_DIM:s * HEAD_DIM + HALF]
        b = pt[s * HEAD_DIM + HALF:(s + 1) * HEAD_DIM]
        parts += [a * cos - b * sin, b * cos + a * sin]
    return jnp.concatenate(parts, axis=0)


def _proj(xb, w_ref, c0, width):
    return jnp.dot(xb, w_ref[:, c0:c0 + width], preferred_element_type=F32)


def _glu_lanes(xb, w_ref, lt):
    r = _proj(xb, w_ref, 3 * D_ATTN + 2 * LANES * lt, 2 * LANES)
    return r[:, :LANES] * jax.nn.sigmoid(r[:, LANES:])


def _glu(xb, w_ref):
    return jnp.concatenate([_glu_lanes(xb, w_ref, lt) for lt in range(D_CONV // LANES)], axis=1)


def _in_proj_rows_kernel(x_ref, w_ref, cos_ref, sin_ref, q_ref, kf_ref, kb_ref, vf_ref, vb_ref, u_ref):
    xb = x_ref[...].astype(BF16)
    q_ref[...] = (_rope_rows(_proj(xb, w_ref, 0, D_ATTN), cos_ref, sin_ref) * Q_SCALE).astype(BF16)
    k = _rope_rows(_proj(xb, w_ref, D_ATTN, D_ATTN), cos_ref, sin_ref)
    kf_ref[...] = k
    kb_ref[...] = k.astype(BF16)
    v = _proj(xb, w_ref, 2 * D_ATTN, D_ATTN)
    vf_ref[...] = v
    vb_ref[...] = v.astype(BF16)
    u_ref[...] = _glu(xb, w_ref)


def _in_proj_rows(x, w_in, cos, sin, layer, tm):
    n = x.shape[0]
    n_tab = cos.shape[0] // tm
    tok = lambda width: pl.BlockSpec((tm, width), lambda i: (i, 0))
    tab = pl.BlockSpec((tm, LANES), lambda i: (i % n_tab, 0))
    return pl.pallas_call(
        _in_proj_rows_kernel,
        grid=(n // tm,),
        in_specs=[tok(D_MODEL), _layer_spec(w_in, layer), tab, tab],
        out_specs=[tok(D_ATTN)] * 5 + [tok(D_CONV)],
        out_shape=[jax.ShapeDtypeStruct((n, D_ATTN), BF16),
                   jax.ShapeDtypeStruct((n, D_ATTN), F32),
                   jax.ShapeDtypeStruct((n, D_ATTN), BF16),
                   jax.ShapeDtypeStruct((n, D_ATTN), F32),
                   jax.ShapeDtypeStruct((n, D_ATTN), BF16),
                   jax.ShapeDtypeStruct((n, D_CONV), F32)],
        compiler_params=_params(1),
        name="in_proj_rows",
    )(x, w_in, cos, sin)


def _in_proj_conv_kernel(x_ref, w_ref, wkt_ref, cos_ref, sin_ref, cost_ref, sint_ref,
                         cw_ref, cb_ref, cg_ref, cbeta_ref, *rest, tiles_per_seq):
    q_ref, ktf_ref, ktb_ref, vf_ref, vb_ref, c_ref, tail_ref, win_sc, pre_sc = rest[-9:]
    tm = x_ref.shape[0]
    first = (pl.program_id(0) % tiles_per_seq) == 0

    @pl.when(first)
    def _():
        win_sc[0:HALO, :] = jnp.zeros((HALO, D_CONV), F32)

    @pl.when(jnp.logical_not(first))
    def _():
        win_sc[0:HALO, :] = win_sc[tm:tm + HALO, :]

    xb = x_ref[...].astype(BF16)

    def proj_q():
        q_ref[...] = _rope_rows(_proj(xb, w_ref, 0, D_ATTN), cos_ref, sin_ref).astype(BF16)

    def proj_k():
        kt = _rope_cols(_dot_nt(wkt_ref[...], xb), cost_ref, sint_ref)
        ktf_ref[...] = kt
        ktb_ref[...] = kt.astype(BF16)

    def proj_v():
        v = _proj(xb, w_ref, 2 * D_ATTN, D_ATTN)
        vb_ref[...] = v.astype(BF16)
        for h in range(N_DIFF_HEADS):
            vf_ref[pl.ds(h, tm, stride=N_DIFF_HEADS), :] = v[:, h * V_DIM:(h + 1) * V_DIM]

    projections = {1: (proj_q, proj_k), 2: (proj_v,)}
    for lt in range(D_CONV // LANES):
        lanes = slice(lt * LANES, (lt + 1) * LANES)
        u = _glu_lanes(xb, w_ref, lt)
        win_sc[HALO:HALO + tm, lanes] = u
        tail_ref[:, lanes] = u[tm - HALO:tm]
        for proj in projections.get(lt, ()):
            proj()
        for base in range(0, tm, TILE_CONV_ROWS):
            pre_sc[base:base + TILE_CONV_ROWS, lanes] = _conv_lanes(win_sc, base, cw_ref, lanes,
                                                                    TILE_CONV_ROWS)
    for rc in range(tm // CONV_ROWS):
        rows = slice(rc * CONV_ROWS, (rc + 1) * CONV_ROWS)
        c_ref[rows, :] = _conv_finish(pre_sc[rows, :], cb_ref, cg_ref, cbeta_ref).astype(BF16)


def _in_proj_conv(x, w_in, wk_t, cos, sin, cos_t, sin_t, conv_params, layer, batch, seq, tm, prev):
    n = x.shape[0]
    nt = seq // tm
    depth = w_in.shape[0]
    tok = lambda width: pl.BlockSpec((tm, width), lambda i: (i, 0))
    tab = pl.BlockSpec((tm, LANES), lambda i: (i % nt, 0))
    tab_t = pl.BlockSpec((HALF, tm), lambda i: (0, i % nt))
    in_specs = ([tok(D_MODEL), _layer_spec(w_in, layer), _layer_spec(wk_t, layer), tab, tab, tab_t, tab_t]
                + [_layer_spec(p, layer) for p in conv_params])
    args = [x, w_in, wk_t, cos, sin, cos_t, sin_t, *conv_params]
    aliases = {}
    if prev is not None:
        in_specs += [pl.BlockSpec(memory_space=pl.ANY)] * 2
        aliases = {len(args): 1, len(args) + 1: 3}
        args += list(prev)
    out_specs = [tok(D_ATTN),
                 pl.BlockSpec((None, None, D_ATTN, tm), lambda i: (layer, i // nt, 0, i % nt)),
                 pl.BlockSpec((None, D_ATTN, tm), lambda i: (i // nt, 0, i % nt)),
                 pl.BlockSpec((None, tm * N_DIFF_HEADS, V_DIM), lambda i: (layer, i, 0)),
                 tok(D_ATTN), tok(D_CONV),
                 pl.BlockSpec((None, HALO, D_CONV), lambda i: (i // nt, 0, 0))]
    out_shape = [jax.ShapeDtypeStruct((n, D_ATTN), BF16),
                 jax.ShapeDtypeStruct((depth, batch, D_ATTN, seq), F32),
                 jax.ShapeDtypeStruct((batch, D_ATTN, seq), BF16),
                 jax.ShapeDtypeStruct((depth, n * N_DIFF_HEADS, V_DIM), F32),
                 jax.ShapeDtypeStruct((n, D_ATTN), BF16),
                 jax.ShapeDtypeStruct((n, D_CONV), BF16),
                 jax.ShapeDtypeStruct((batch, HALO, D_CONV), F32)]
    return pl.pallas_call(
        functools.partial(_in_proj_conv_kernel, tiles_per_seq=nt),
        grid=(n // tm,),
        in_specs=in_specs,
        out_specs=out_specs,
        out_shape=out_shape,
        input_output_aliases=aliases,
        scratch_shapes=[pltpu.VMEM((HALO + tm, D_CONV), F32), pltpu.VMEM((tm, D_CONV), F32)],
        compiler_params=_params(1),
        name="in_proj_conv",
    )(*args)


def _stack_sub_heads(q):
    lane = lax.broadcasted_iota(jnp.int32, q.shape, 1)
    zero = jnp.zeros_like(q)
    return jnp.concatenate([jnp.where(lane < HEAD_DIM, q, zero),
                            jnp.where(lane >= HEAD_DIM, q, zero)], axis=0)


def _lambda(lq1, lk1, lq2, lk2, lam_init):
    s1 = jnp.sum(lq1[...] * lk1[...], axis=1, keepdims=True)
    s2 = jnp.sum(lq2[...] * lk2[...], axis=1, keepdims=True)
    return jnp.exp(s1) - jnp.exp(s2) + lam_init


def _diff_finish(o_all, lam, g, lam_init):
    t = o_all.shape[0] // 2
    o = o_all[:t] - lam * o_all[t:]
    o = o * lax.rsqrt(jnp.mean(o * o, axis=-1, keepdims=True) + RMS_EPS)
    return o * g * (1.0 - lam_init)


def _row_max(s):
    m = s[:, 0:LANES]
    for c in range(1, s.shape[1] // LANES):
        m = jnp.maximum(m, s[:, c * LANES:(c + 1) * LANES])
    return jnp.broadcast_to(jnp.max(m, axis=1, keepdims=True), m.shape)


def _with_ones(v):
    return jnp.concatenate([v, jnp.ones(v.shape, v.dtype)], axis=1)


def _wide(a, width):
    return jnp.concatenate([a] * (width // LANES), axis=1)


def _attn_prompt_kernel(lq1, lk1, lq2, lk2, g_ref, q_ref, kt_ref, v_ref, o_ref, *scratch,
                        lam_init, tq, n_q):
    m_scs = scratch[:N_DIFF_HEADS]
    acc_scs = scratch[N_DIFF_HEADS:]
    qi = pl.program_id(1)
    heads = [slice(h * V_DIM, (h + 1) * V_DIM) for h in range(N_DIFF_HEADS)]
    qqs = [_stack_sub_heads(q_ref[:, cols]) for cols in heads]
    lam = _lambda(lq1, lk1, lq2, lk2, lam_init)
    single_step = n_q <= WIDE_BLOCKS

    def finish(cols, acc):
        o_all = acc[:, :V_DIM] / acc[:, V_DIM:]
        o_ref[:, cols] = _diff_finish(o_all, lam, g_ref[...], lam_init).astype(o_ref.dtype)

    def step(start, width, diagonal):
        if diagonal:
            shape = (2 * tq, tq)
            q_chunk = (lax.broadcasted_iota(jnp.int32, shape, 0) % tq) // CHUNK
            k_chunk = lax.broadcasted_iota(jnp.int32, shape, 1) // CHUNK
            visible = k_chunk <= q_chunk
        scores = [jnp.dot(qq, kt_ref[cols, pl.ds(start, width)], preferred_element_type=F32)
                  for cols, qq in zip(heads, qqs)]
        for cols, s, m_sc, acc_sc in zip(heads, scores, m_scs, acc_scs):
            v = v_ref[pl.ds(start, width), cols]
            if diagonal:
                last = jnp.where(visible, s[:, width - tq:], NEG_INF)
                s = last if width == tq else jnp.concatenate([s[:, :width - tq], last], axis=1)
                m_new = _row_max(s)
            else:
                m_prev = m_sc[...]
                m_new = jnp.maximum(m_prev, _row_max(s))
            p = jnp.exp2(s - _wide(m_new, width)).astype(BF16)
            pv = jnp.dot(p, _with_ones(v), preferred_element_type=F32)
            if diagonal and single_step:
                finish(cols, pv)
                continue
            if diagonal:
                acc_sc[...] = pv
            else:
                acc_sc[...] = _wide(jnp.exp2(m_prev - m_new), 2 * V_DIM) * acc_sc[...] + pv
            m_sc[...] = m_new

    for extra in range(min(WIDE_BLOCKS, n_q)):
        @pl.when(qi % WIDE_BLOCKS == extra)
        def _():
            step(pl.multiple_of((qi - extra) * tq, tq), (extra + 1) * tq, True)

    if not single_step:
        def body(kb, carry):
            step(pl.multiple_of(kb * WIDE_BLOCKS * tq, WIDE_BLOCKS * tq), WIDE_BLOCKS * tq, False)
            return carry

        lax.fori_loop(0, qi // WIDE_BLOCKS, body, 0)
        for cols, acc_sc in zip(heads, acc_scs):
            finish(cols, acc_sc[...])


def _attn_prompt(q, kt, v, attn_params, layer, batch, seq, tq):
    nq = seq // tq
    kt_spec = pl.BlockSpec((None, D_ATTN, seq), lambda b, i: (b, 0, 0))
    v_spec = pl.BlockSpec((seq, D_ATTN), lambda b, i: (b, 0))
    q_spec = pl.BlockSpec((tq, D_ATTN), lambda b, i: (b * nq + i, 0))
    return pl.pallas_call(
        functools.partial(_attn_prompt_kernel, lam_init=_lambda_init(layer), tq=tq, n_q=nq),
        grid=(batch, nq),
        in_specs=[_layer_spec(p, layer) for p in attn_params] + [q_spec, kt_spec, v_spec],
        out_specs=q_spec,
        out_shape=jax.ShapeDtypeStruct((batch * seq, D_ATTN), BF16),
        scratch_shapes=([pltpu.VMEM((2 * tq, LANES), F32)] * N_DIFF_HEADS
                        + [pltpu.VMEM((2 * tq, 2 * V_DIM), F32)] * N_DIFF_HEADS),
        compiler_params=_params(2),
        name="attn_prompt",
    )(*attn_params, q, kt, v)


def _attn_sample_kernel(lq1, lk1, lq2, lk2, g_ref, q_ref, kn_ref, vn_ref, ckt_hbm, cv_hbm, o_ref,
                        ckt_buf, cv_buf, sem, *, lam_init, past, layer, n_b):
    b = pl.program_id(0)
    slot = b % 2

    def copies(i, s):
        return (pltpu.make_async_copy(ckt_hbm.at[layer, i], ckt_buf.at[s], sem.at[0, s]),
                pltpu.make_async_copy(cv_hbm.at[layer, i], cv_buf.at[s], sem.at[1, s]))

    def start(i, s):
        k_copy, v_copy = copies(i, s)
        k_copy.start()
        v_copy.start(priority=1)

    @pl.when(b == 0)
    def _():
        start(0, 0)

    @pl.when(b + 1 < n_b)
    def _():
        start(b + 1, 1 - slot)

    for copy in copies(b, slot):
        copy.wait()
    ckt_ref = ckt_buf.at[slot]
    cv_ref = cv_buf.at[slot]
    lam = _lambda(lq1, lk1, lq2, lk2, lam_init)
    for h in range(N_DIFF_HEADS):
        cols = slice(h * V_DIM, (h + 1) * V_DIM)
        qq = _stack_sub_heads(q_ref[:, cols])
        s_past = jnp.dot(qq, ckt_ref[cols, :].astype(BF16), preferred_element_type=F32)
        s_new = _dot_nt(qq, kn_ref[:, cols])
        m = jnp.maximum(_row_max(s_past), jnp.max(s_new, axis=1, keepdims=True))
        p_past = jnp.exp2(s_past - _wide(m, past)).astype(BF16)
        p_new = jnp.exp2(s_new - m[:, :s_new.shape[1]]).astype(BF16)
        v_past = cv_ref[pl.ds(h, past, stride=N_DIFF_HEADS), :].astype(BF16)
        acc = (jnp.dot(p_past, _with_ones(v_past), preferred_element_type=F32)
               + jnp.dot(p_new, _with_ones(vn_ref[:, cols]), preferred_element_type=F32))
        o_all = acc[:, :V_DIM] / acc[:, V_DIM:]
        o_ref[:, cols] = _diff_finish(o_all, lam, g_ref[...], lam_init).astype(o_ref.dtype)


def _attn_sample(q, k, v, cache_kt, cache_v, attn_params, layer, batch, seq, past):
    new_spec = pl.BlockSpec((seq, D_ATTN), lambda b: (b, 0))
    return pl.pallas_call(
        functools.partial(_attn_sample_kernel, lam_init=_lambda_init(layer), past=past, layer=layer,
                          n_b=batch),
        grid=(batch,),
        in_specs=[_layer_spec(p, layer) for p in attn_params]
                 + [new_spec, new_spec, new_spec,
                    pl.BlockSpec(memory_space=pl.ANY), pl.BlockSpec(memory_space=pl.ANY)],
        out_specs=new_spec,
        out_shape=jax.ShapeDtypeStruct((batch * seq, D_ATTN), BF16),
        scratch_shapes=[pltpu.VMEM((2, D_ATTN, past), F32),
                        pltpu.VMEM((2, past * N_DIFF_HEADS, V_DIM), F32),
                        pltpu.SemaphoreType.DMA((2, 2))],
        compiler_params=_params(1),
        name="attn_sample",
    )(*attn_params, q, k, v, cache_kt, cache_v)


def _post_kernel(attn_ref, c_ref, x_ref, wo_ref, g1_ref, b1_ref, w1_ref, w2_ref, g2_ref, b2_ref, o_ref):
    tm = x_ref.shape[0]
    halves = [slice(i * (tm // POST_CHAINS), (i + 1) * (tm // POST_CHAINS)) for i in range(POST_CHAINS)]
    n_ff = D_FF // FF_CHUNK

    def out_proj(rows):
        return (jnp.dot(attn_ref[rows, :], wo_ref[0:D_ATTN, :], preferred_element_type=F32)
                + jnp.dot(c_ref[rows, :], wo_ref[D_ATTN:D_ATTN + D_CONV, :], preferred_element_type=F32))

    def ff(c, xb, acc):
        h = jnp.dot(xb, w1_ref[:, c * FF_CHUNK:(c + 1) * FF_CHUNK], preferred_element_type=F32)
        h = jnp.maximum(h, 0.0)
        h = (h * h).astype(BF16)
        return acc + jnp.dot(h, w2_ref[c * FF_CHUNK:(c + 1) * FF_CHUNK, :], preferred_element_type=F32)

    ys = [out_proj(rows) for rows in halves]
    x1s = [_layer_norm(DEEPNORM_ALPHA * x_ref[rows, :] + y, g1_ref[...], b1_ref[...])
           for rows, y in zip(halves, ys)]
    xbs = [x1.astype(BF16) for x1 in x1s]
    accs = [jnp.zeros(x1.shape, F32) for x1 in x1s]
    for c in range(n_ff):
        for i in range(POST_CHAINS):
            accs[i] = ff(c, xbs[i], accs[i])
            if c == n_ff - 1:
                o_ref[halves[i], :] = _layer_norm(DEEPNORM_ALPHA * x1s[i] + accs[i], g2_ref[...], b2_ref[...])


def _post(attn, c, x, post_params, layer, tm):
    n = x.shape[0]
    tok = lambda width: pl.BlockSpec((tm, width), lambda i: (i, 0))
    return pl.pallas_call(
        _post_kernel,
        grid=(n // tm,),
        in_specs=[tok(D_ATTN), tok(D_CONV), tok(D_MODEL)] + [_layer_spec(p, layer) for p in post_params],
        out_specs=tok(D_MODEL),
        out_shape=jax.ShapeDtypeStruct((n, D_MODEL), F32),
        compiler_params=_params(1),
        name="post",
    )(attn, c, x, *post_params)


def _rope_angles(pos):
    inv = ROPE_THETA ** (-jnp.arange(HALF, dtype=F32) / HALF)
    ang = pos.astype(F32)[:, None] * inv[None, :]
    return jnp.cos(ang), jnp.sin(ang)


def _rope_tables(pos):
    cos, sin = _rope_angles(pos)
    return (jnp.concatenate([cos, cos, cos, cos], axis=1),
            jnp.concatenate([-sin, sin, -sin, sin], axis=1))


def _rows(a):
    return a.reshape(a.shape[0], 1, a.shape[1])


TM = 512
TQ = 256
WIDE_BLOCKS = 8
POST_CHAINS = 2


def kernel(x_prompt, x_sample, cache_k, cache_v, cache_conv, w_in, lambda_q1, lambda_k1, lambda_q2,
           lambda_k2, subln_g, conv_w, conv_b, conv_ln_g, conv_ln_b, w_out, ln1_g, ln1_b, w_ff1,
           w_ff2, ln2_g, ln2_b):
    B, T, _ = x_prompt.shape
    Bs, Ts, _ = x_sample.shape
    P = cache_k.shape[2]
    depth = w_in.shape[0]

    cols = [w_in[:, :, :3 * D_ATTN]]
    for lt in range(D_CONV // LANES):
        for c0 in (3 * D_ATTN, 3 * D_ATTN + D_CONV):
            cols.append(w_in[:, :, c0 + lt * LANES:c0 + (lt + 1) * LANES])
    w_in_b = jnp.concatenate(cols, axis=2).astype(BF16)
    wk_t = jnp.swapaxes(w_in_b[:, :, D_ATTN:2 * D_ATTN], 1, 2)
    w_out_b = w_out.astype(BF16)
    w_ff1_b = w_ff1.astype(BF16)
    w_ff2_b = w_ff2.astype(BF16)

    pos_p = jnp.arange(T)
    cos_p, sin_p = (a * Q_SCALE for a in _rope_tables(pos_p))
    cos_pt, sin_pt = (a.T for a in _rope_angles(pos_p))
    cos_s, sin_s = _rope_tables(P + jnp.arange(Ts))
    reps = TM // Ts
    cos_s = jnp.tile(cos_s, (reps, 1))
    sin_s = jnp.tile(sin_s, (reps, 1))

    xp = x_prompt.reshape(B * T, D_MODEL)
    xs = x_sample.reshape(Bs * Ts, D_MODEL)
    cache_kt = jnp.transpose(cache_k, (0, 1, 3, 4, 2)).reshape(depth, Bs, D_ATTN, P)
    cache_vr = cache_v.reshape(depth, Bs, P * N_DIFF_HEADS, V_DIM)
    hist_s = jnp.pad(cache_conv, ((0, 0), (0, 0), (HALO - CONV_HIST, 0), (0, 0)))
    hist_s = hist_s.reshape(depth, Bs * HALO, D_CONV)

    attn_params = tuple(_rows(a) for a in (lambda_q1, lambda_k1, lambda_q2, lambda_k2, subln_g))
    conv_params = (conv_w, _rows(conv_b), _rows(conv_ln_g), _rows(conv_ln_b))
    post_params = (w_out_b, _rows(ln1_g), _rows(ln1_b), w_ff1_b, w_ff2_b, _rows(ln2_g), _rows(ln2_b))

    outs = {k: [] for k in ("cp", "ks", "vs", "cs")}
    kv_prompt = None
    for l in range(depth):
        q, ktf, ktb, vf, vb, c, tail = _in_proj_conv(xp, w_in_b, wk_t, cos_p, sin_p, cos_pt, sin_pt,
                                                     conv_params, l, B, T, TM, kv_prompt)
        kv_prompt = (ktf, vf)
        attn = _attn_prompt(q, ktb, vb, attn_params, l, B, T, TQ)
        xp = _post(attn, c, xp, post_params, l, TM)
        outs["cp"].append(tail[:, HALO - CONV_HIST:])

        q, kf, kb, vf_s, vb, u = _in_proj_rows(xs, w_in_b, cos_s, sin_s, l, TM)
        attn = _attn_sample(q, kb, vb, cache_kt, cache_vr, attn_params, l, Bs, Ts, P)
        c = _conv_rows(u, hist_s, conv_params, l, Ts, TM)
        xs = _post(attn, c, xs, post_params, l, TM)
        outs["ks"].append(kf.reshape(Bs, Ts, N_SUB_HEADS, HEAD_DIM))
        outs["vs"].append(vf_s.reshape(Bs, Ts, N_DIFF_HEADS, V_DIM))
        outs["cs"].append(u.reshape(Bs, Ts, D_CONV)[:, Ts - CONV_HIST:])

    ktf, vf = kv_prompt
    new_k_prompt = jnp.transpose(ktf.reshape(depth, B, N_SUB_HEADS, HEAD_DIM, T), (0, 1, 4, 2, 3))
    new_v_prompt = vf.reshape(depth, B, T, N_DIFF_HEADS, V_DIM)
    return (xp.reshape(B, T, D_MODEL), xs.reshape(Bs, Ts, D_MODEL),
            new_k_prompt, new_v_prompt, jnp.stack(outs["cp"]),
            jnp.stack(outs["ks"]), jnp.stack(outs["vs"]), jnp.stack(outs["cs"]))
```
